```python
import jax, jax.numpy as jnp
from jax import lax
import numpy as np

D_MODEL = 2048
BATCH = 2
SEQ = 16384
DEPTH = 2

GRID_W = 64
CTX_LEN = 256
MIX_WIDTH = D_MODEL
HGRN_WIDTH = MIX_WIDTH // 2
POOL_WIDTH = MIX_WIDTH - HGRN_WIDTH
HGRN_HEAD_DIM = 128
HGRN_HEADS = HGRN_WIDTH // HGRN_HEAD_DIM
HGRN_CHUNK = 64
POOL_WINDOWS = (2, 4, 8, 16)
POOL_GROUPS = len(POOL_WINDOWS)
POOL_GROUP_DIM = POOL_WIDTH // POOL_GROUPS
D_FF = 256 * ((8 * D_MODEL // 3 + 255) // 256)
N_MOD = 9
IN_WIDTH = 5 * HGRN_WIDTH + POOL_WIDTH
EPS = 1e-6
LOG_ZERO = -1e30

kernel_name = "hybrid_hgrn2_pool_macaron_dit"


def rms_norm(x, gain):
    xf = x.astype(jnp.float32)
    y = xf * lax.rsqrt(jnp.mean(xf * xf, axis=-1, keepdims=True) + EPS)
    return (y * gain.astype(jnp.float32)).astype(x.dtype)


def modulate(h, shift, scale):
    return h * (1 + scale) + shift


def swiglu(h, w_in, w_out):
    gate, up = jnp.split(h @ w_in, 2, axis=-1)
    return (jax.nn.silu(gate) * up) @ w_out


def ffn_sublayer(x, mod, j, gains, w_in, w_out):
    h = modulate(rms_norm(x, gains[2 * j]), mod[:, 3 * j, None, :], mod[:, 3 * j + 1, None, :])
    y = rms_norm(swiglu(h, w_in, w_out), gains[2 * j + 1])
    return x + 0.5 * mod[:, 3 * j + 2, None, :] * y


def gla_chunk_scan(q, k, v, log_f, s0):
    bsz, L, H, dk = q.shape
    dv = v.shape[-1]
    n = L // HGRN_CHUNK
    to_chunks = lambda a: jnp.moveaxis(a.reshape(bsz, n, HGRN_CHUNK, H, a.shape[-1]), 1, 0)
    mask = jnp.tril(jnp.ones((HGRN_CHUNK, HGRN_CHUNK), dtype=bool))

    def step(S, inp):
        qc, kc, vc, gc = inp
        b = jnp.cumsum(gc, axis=1)
        o_inter = jnp.einsum('bthk,bhkv->bthv', qc * jnp.exp(b), S)
        diff = b[:, :, None] - b[:, None, :]
        decay = jnp.where(mask[None, :, :, None, None], jnp.exp(jnp.minimum(diff, 0.0)), 0.0)
        attn = jnp.sum(qc[:, :, None] * kc[:, None] * decay, axis=-1)
        o_intra = jnp.einsum('btsh,bshv->bthv', attn, vc)
        b_end = b[:, -1]
        S_new = jnp.exp(b_end)[..., None] * S + jnp.einsum(
            'bshk,bshv->bhkv', kc * jnp.exp(b_end[:, None] - b), vc)
        return S_new, o_inter + o_intra

    s_final, o = lax.scan(step, s0, (to_chunks(q), to_chunks(k), to_chunks(v), to_chunks(log_f)))
    o = jnp.moveaxis(o, 0, 1).reshape(bsz, L, H, dv)
    return o, s_final


def hgrn2_direction(qh, vh, f_logit, lb, s0, reverse):
    bsz, L = f_logit.shape[:2]
    shp = (bsz, L, HGRN_HEADS, HGRN_HEAD_DIM)
    pos = lb > 0
    log_lb = jnp.where(pos, jnp.log(jnp.where(pos, lb, 1.0)), LOG_ZERO)
    log_f = jnp.logaddexp(jax.nn.log_sigmoid(f_logit),
                          log_lb + jax.nn.log_sigmoid(-f_logit)).reshape(shp)
    kh = ((1 - lb) * jax.nn.sigmoid(-f_logit)).reshape(shp)
    if reverse:
        qh, kh, vh, log_f = (jnp.flip(a, axis=1) for a in (qh, kh, vh, log_f))
    o, s = gla_chunk_scan(qh, kh, vh, log_f, s0)
    if reverse:
        o = jnp.flip(o, axis=1)
    return o, s


def hgrn2_mixer(q, f_fw, f_bw, iv, g, lb, gain, s0_fw, s0_bw):
    bsz, L, _ = q.shape
    shp = (bsz, L, HGRN_HEADS, HGRN_HEAD_DIM)
    qh = jax.nn.silu(q.astype(jnp.float32)).reshape(shp)
    vh = iv.astype(jnp.float32).reshape(shp)
    o_fw, s_fw = hgrn2_direction(qh, vh, f_fw.astype(jnp.float32), lb[0], s0_fw, False)
    o_bw, s_bw = hgrn2_direction(qh, vh, f_bw.astype(jnp.float32), lb[1], s0_bw, True)
    o = rms_norm(o_fw + o_bw, gain).reshape(bsz, L, HGRN_WIDTH)
    o = o * jax.nn.silu(g.astype(jnp.float32))
    return o.astype(q.dtype), s_fw, s_bw


def box_mean(v, w, axis):
    n = v.shape[axis]
    pad = [(0, 0)] * v.ndim
    pad[axis] = (1, 0)
    cs = jnp.pad(jnp.cumsum(v, axis=axis), pad)
    t = jnp.arange(n)
    lo = jnp.clip(t - w // 2, 0, n)
    hi = jnp.clip(t + w - w // 2, 0, n)
    s = jnp.take(cs, hi, axis=axis) - jnp.take(cs, lo, axis=axis)
    cnt_shape = [1] * v.ndim
    cnt_shape[axis] = n
    return s / (hi - lo).astype(v.dtype).reshape(cnt_shape)


def pool_mixer(p, rows, w_pool, b_pool, pool_scale):
    bsz, L, _ = p.shape
    groups = p.astype(jnp.float32).reshape(bsz, L, POOL_GROUPS, POOL_GROUP_DIM)
    outs = []
    for gi, w in enumerate(POOL_WINDOWS):
        v = groups[:, :, gi]
        if rows > 0:
            vg = v.reshape(bsz, rows, GRID_W, POOL_GROUP_DIM)
            m = box_mean(box_mean(vg, w, 1), w, 2).reshape(bsz, L, POOL_GROUP_DIM)
        else:
            m = box_mean(v, w, 1)
        outs.append(m - v)
    d = jnp.stack(outs, axis=2).astype(p.dtype)
    y = jnp.einsum('blgc,gcd->blgd', d, w_pool).reshape(bsz, L, POOL_WIDTH) + b_pool
    return y * pool_scale


def setup_inputs(seed: int = 0) -> dict:
    key = jax.random.key(seed)
    ks = jax.random.split(key, 16)
    f32 = jnp.float32
    nrm = lambda k, shp, s: jax.random.normal(k, shp, f32) * s
    return {
        "x": nrm(ks[0], (BATCH, SEQ, D_MODEL), 1.0),
        "c": nrm(ks[1], (BATCH, D_MODEL), 1.0),
        "ctx": nrm(ks[2], (BATCH, CTX_LEN, D_MODEL), 1.0),
        "c_ctx": nrm(ks[3], (D_MODEL,), 1.0),
        "w_ada": nrm(ks[4], (DEPTH, D_MODEL, N_MOD * D_MODEL), 0.5 * D_MODEL ** -0.5),
        "b_ada": nrm(ks[5], (DEPTH, N_MOD * D_MODEL), 0.02),
        "norm_gain": 1.0 + nrm(ks[6], (DEPTH, 6, D_MODEL), 0.05),
        "ffn_in": nrm(ks[7], (DEPTH, 2, D_MODEL, 2 * D_FF), D_MODEL ** -0.5),
        "ffn_out": nrm(ks[8], (DEPTH, 2, D_FF, D_MODEL), D_FF ** -0.5),
        "w_in": nrm(ks[9], (DEPTH, D_MODEL, IN_WIDTH), D_MODEL ** -0.5),
        "hgrn_lb": nrm(ks[10], (DEPTH, 2, HGRN_WIDTH), 0.5),
        "hgrn_gain": 1.0 + nrm(ks[11], (DEPTH, HGRN_HEAD_DIM), 0.05),
        "w_pool": nrm(ks[12], (DEPTH, POOL_GROUPS, POOL_GROUP_DIM, POOL_GROUP_DIM), POOL_GROUP_DIM ** -0.5),
        "b_pool": nrm(ks[13], (DEPTH, POOL_WIDTH), 0.02),
        "pool_scale": 1.0 + nrm(ks[14], (DEPTH, POOL_WIDTH), 0.1),
        "w_out": nrm(ks[15], (DEPTH, MIX_WIDTH, D_MODEL), MIX_WIDTH ** -0.5),
    }


def reference(x, c, ctx, c_ctx, w_ada, b_ada, norm_gain, ffn_in, ffn_out, w_in, hgrn_lb,
              hgrn_gain, w_pool, b_pool, pool_scale, w_out):
    bsz, seq = x.shape[0], x.shape[1]
    rows = seq // GRID_W
    split_at = [HGRN_WIDTH * k for k in range(1, 6)]
    lbp = jax.nn.softmax(hgrn_lb.astype(jnp.float32), axis=0)
    lower_bounds = jnp.cumsum(lbp, axis=0) - lbp[0]
    s0 = jnp.zeros((bsz, HGRN_HEADS, HGRN_HEAD_DIM, HGRN_HEAD_DIM), jnp.float32)
    for l in range(DEPTH):
        last = l == DEPTH - 1
        gains = norm_gain[l]
        mod_x = (jax.nn.silu(c) @ w_ada[l] + b_ada[l]).reshape(bsz, N_MOD, D_MODEL)
        mod_c = (jax.nn.silu(c_ctx)[None] @ w_ada[l] + b_ada[l]).reshape(1, N_MOD, D_MODEL)

        x = ffn_sublayer(x, mod_x, 0, gains, ffn_in[l, 0], ffn_out[l, 0])
        ctx = ffn_sublayer(ctx, mod_c, 0, gains, ffn_in[l, 0], ffn_out[l, 0])

        h_x = modulate(rms_norm(x, gains[2]), mod_x[:, 3, None, :], mod_x[:, 4, None, :])
        h_c = modulate(rms_norm(ctx, gains[2]), mod_c[:, 3, None, :], mod_c[:, 4, None, :])
        qx, ffx, fbx, ix, gx, px = jnp.split(h_x @ w_in[l], split_at, axis=-1)
        qc, ffc, fbc, ic, gc, pc = jnp.split(h_c @ w_in[l], split_at, axis=-1)
        o_c, s_fw, s_bw = hgrn2_mixer(qc, ffc, fbc, ic, gc, lower_bounds[l], hgrn_gain[l], s0, s0)
        o_x, _, _ = hgrn2_mixer(qx, ffx, fbx, ix, gx, lower_bounds[l], hgrn_gain[l], s_fw, s_bw)
        mix_x = jnp.concatenate(
            [o_x, pool_mixer(px, rows, w_pool[l], b_pool[l], pool_scale[l])], axis=-1) @ w_out[l]
        x = x + mod_x[:, 5, None, :] * rms_norm(mix_x, gains[3])
        if not last:
            mix_c = jnp.concatenate(
                [o_c, pool_mixer(pc, 0, w_pool[l], b_pool[l], pool_scale[l])], axis=-1) @ w_out[l]
            ctx = ctx + mod_c[:, 5, None, :] * rms_norm(mix_c, gains[3])
            ctx = ffn_sublayer(ctx, mod_c, 2, gains, ffn_in[l, 1], ffn_out[l, 1])

        x = ffn_sublayer(x, mod_x, 2, gains, ffn_in[l, 1], ffn_out[l, 1])
    return x
```

```python
import functools

import jax
import jax.numpy as jnp
from jax import lax
from jax.experimental import pallas as pl
from jax.experimental.pallas import tpu as pltpu

F32 = jnp.float32
BF16 = jnp.bfloat16

HEAD_DIM = 128
GRID_W = 64
POOL_WINDOWS = (2, 4, 8, 16)
POOL_GROUP_DIM = 256
N_MOD = 9
EPS = 1e-6
LOG_ZERO = -1e30
V7X_VMEM_LIMIT_BYTES = 56 * 1024 * 1024
MOD_ROWS = 8
HGRN_CHUNK = 256
POOL_TILE_ROWS = 16
POOL_HALO_ROWS = 8
COL_BLOCK = 256
ROW_CHUNK = 256


def _params(*semantics):
    return pltpu.CompilerParams(dimension_semantics=semantics,
                                vmem_limit_bytes=V7X_VMEM_LIMIT_BYTES)


def _sigmoid(x):
    return 1.0 / (1.0 + jnp.exp(-x))


def _rms(x):
    return x * lax.rsqrt(jnp.mean(x * x, axis=-1, keepdims=True) + EPS)


def _mod_kernel(c_ref, w_ref, b_ref, o_ref):
    c = c_ref[...]
    s = (c * _sigmoid(c)).astype(BF16)
    o_ref[0] = jnp.dot(s, w_ref[0].astype(BF16), preferred_element_type=F32) + b_ref[0]


def _mod_call(cvec, w_ada, b_ada, tn=1024):
    depth, d, n = w_ada.shape
    return pl.pallas_call(
        _mod_kernel,
        out_shape=jax.ShapeDtypeStruct((depth, MOD_ROWS, n), F32),
        grid=(depth, n // tn),
        in_specs=[
            pl.BlockSpec((MOD_ROWS, d), lambda l, i: (0, 0)),
            pl.BlockSpec((1, d, tn), lambda l, i: (l, 0, i)),
            pl.BlockSpec((1, 1, tn), lambda l, i: (l, 0, i)),
        ],
        out_specs=pl.BlockSpec((1, MOD_ROWS, tn), lambda l, i: (l, 0, i)),
        compiler_params=_params("parallel", "parallel"),
        name="adaln_mod",
    )(cvec, w_ada, b_ada.reshape(depth, 1, n))


def _for_row_chunks(n_rows, body):
    chunk = min(ROW_CHUNK, n_rows)

    def step(r, carry):
        body(pl.ds(pl.multiple_of(r * chunk, chunk), chunk))
        return carry

    lax.fori_loop(0, n_rows // chunk, step, 0)


def _ffn_kernel(x_ref, mod_ref, g_ref, wg_ref, wu_ref, wo_ref, o_ref, h_ref, *, j):
    f = pl.program_id(2)
    tm = h_ref.shape[0]

    @pl.when(f == 0)
    def _():
        def prologue(rows):
            y = _rms(x_ref[0, rows, :]) * g_ref[pl.ds(2 * j, 1), :]
            shift = mod_ref[0, pl.ds(3 * j, 1), :]
            scale = mod_ref[0, pl.ds(3 * j + 1, 1), :]
            h_ref[rows, :] = (y * (1.0 + scale) + shift).astype(BF16)
            o_ref[0, rows, :] = jnp.zeros((rows.size, o_ref.shape[2]), F32)

        _for_row_chunks(tm, prologue)

    h = h_ref[...]
    gate = jnp.dot(h, wg_ref[...], preferred_element_type=F32)
    up = jnp.dot(h, wu_ref[...], preferred_element_type=F32)
    a = (gate * _sigmoid(gate) * up).astype(BF16)
    o_ref[0] += jnp.dot(a, wo_ref[...], preferred_element_type=F32)

    @pl.when(f == pl.num_programs(2) - 1)
    def _():
        def epilogue(rows):
            y = _rms(o_ref[0, rows, :]) * g_ref[pl.ds(2 * j + 1, 1), :]
            o_ref[0, rows, :] = x_ref[0, rows, :] + 0.5 * mod_ref[0, pl.ds(3 * j + 2, 1), :] * y

        _for_row_chunks(tm, epilogue)


def _ffn_call(x, mod, gains, w_in, w_out, j, tf=256):
    b, l, d = x.shape
    d_ff = w_out.shape[0]
    tm = min(1024, l)
    nf = d_ff // tf
    return pl.pallas_call(
        functools.partial(_ffn_kernel, j=j),
        out_shape=jax.ShapeDtypeStruct((b, l, d), F32),
        grid=(b, l // tm, nf),
        in_specs=[
            pl.BlockSpec((1, tm, d), lambda bi, i, f: (bi, i, 0)),
            pl.BlockSpec((1, N_MOD, d), lambda bi, i, f: (bi, 0, 0)),
            pl.BlockSpec(gains.shape, lambda bi, i, f: (0, 0)),
            pl.BlockSpec((d, tf), lambda bi, i, f: (0, f)),
            pl.BlockSpec((d, tf), lambda bi, i, f: (0, f + nf)),
            pl.BlockSpec((tf, d), lambda bi, i, f: (f, 0)),
        ],
        out_specs=pl.BlockSpec((1, tm, d), lambda bi, i, f: (bi, i, 0)),
        scratch_shapes=[pltpu.VMEM((tm, d), BF16)],
        compiler_params=_params("parallel", "parallel", "arbitrary"),
        name="ffn_sublayer",
    )(x, mod, gains, w_in, w_in, w_out)


def _inproj_kernel(x_ref, mod_ref, g_ref, w_ref, z_ref, h_ref):
    @pl.when(pl.program_id(2) == 0)
    def _():
        def prologue(rows):
            y = _rms(x_ref[0, rows, :]) * g_ref[pl.ds(2, 1), :]
            shift = mod_ref[0, pl.ds(3, 1), :]
            scale = mod_ref[0, pl.ds(4, 1), :]
            h_ref[rows, :] = (y * (1.0 + scale) + shift).astype(BF16)

        _for_row_chunks(h_ref.shape[0], prologue)

    z_ref[0] = jnp.dot(h_ref[...], w_ref[...], preferred_element_type=F32)


def _inproj_call(x, mod, gains, w, tn=1024):
    b, l, d = x.shape
    n = w.shape[1]
    tm = min(1024, l)
    return pl.pallas_call(
        _inproj_kernel,
        out_shape=jax.ShapeDtypeStruct((b, l, n), F32),
        grid=(b, l // tm, n // tn),
        in_specs=[
            pl.BlockSpec((1, tm, d), lambda bi, i, k: (bi, i, 0)),
            pl.BlockSpec((1, N_MOD, d), lambda bi, i, k: (bi, 0, 0)),
            pl.BlockSpec(gains.shape, lambda bi, i, k: (0, 0)),
            pl.BlockSpec((d, tn), lambda bi, i, k: (0, k)),
        ],
        out_specs=pl.BlockSpec((1, tm, tn), lambda bi, i, k: (bi, i, k)),
        scratch_shapes=[pltpu.VMEM((tm, d), BF16)],
        compiler_params=_params("parallel", "parallel", "arbitrary"),
        name="mixer_in_proj",
    )(x, mod, gains, w)


def _shift_rows(x, s):
    n = x.shape[0]
    s = s % n
    if s % 8 == 0:
        return jnp.concatenate([x[n - s:], x[:n - s]], axis=0)
    return pltpu.roll(x, s, 0)


def _hgrn_direction(q_raw, z, v, log_lb, one_m_lb, st, reverse):
    c = q_raw.shape[0]
    e = jnp.exp(-jnp.abs(z))
    t = jnp.log1p(e)
    log_sig = jnp.minimum(z, 0.0) - t
    other = log_lb + jnp.minimum(-z, 0.0) - t
    log_f = jnp.maximum(log_sig, other) + jnp.log1p(jnp.exp(-jnp.abs(z - log_lb)))
    k = one_m_lb * (jnp.where(z > 0, e, 1.0) / (1.0 + e))
    q = q_raw * _sigmoid(q_raw)

    row = lax.broadcasted_iota(jnp.int32, (c, HEAD_DIM), 0)
    ti = lax.broadcasted_iota(jnp.int32, (c, c), 0)
    si = lax.broadcasted_iota(jnp.int32, (c, c), 1)
    later = (ti < si) if reverse else (ti > si)
    code = jnp.where(later, ti ^ si, 0)

    zeros = jnp.zeros_like(log_f)
    p, x = (zeros, log_f) if reverse else (log_f, zeros)
    attn = jnp.zeros((c, c), F32)
    half = 1
    while half < c:
        odd = (row & half) != 0
        w = jnp.exp(jnp.where(odd, p, x))
        a = lax.dot_general((q * w).astype(BF16), (k * w).astype(BF16),
                            (((1,), (1,)), ((), ())), preferred_element_type=F32)
        attn = jnp.where(code >= half, a, attn)
        tot = p + x
        p = jnp.where(odd, p + _shift_rows(tot, half), p)
        x = jnp.where(odd, x, x + _shift_rows(tot, c - half))
        half *= 2

    q_in, k_out = (x, p) if reverse else (p, x)
    q_state = (q * jnp.exp(q_in)).astype(BF16)
    k_state = (k * jnp.exp(k_out)).astype(BF16)
    vb = v.astype(BF16)
    o = jnp.dot(attn.astype(BF16), vb, preferred_element_type=F32)
    o += lax.dot_general(q_state, st.astype(BF16), (((1,), (1,)), ((), ())),
                         preferred_element_type=F32)
    o += jnp.sum(q * k, axis=-1, keepdims=True) * v
    st_new = st * jnp.exp((p + x)[0:1, :]) + lax.dot_general(
        vb, k_state, (((0,), (0,)), ((), ())), preferred_element_type=F32)
    return o, st_new


def _hgrn_kernel(qf_ref, ff_ref, vf_ref, qb_ref, fb_ref, vb_ref, loglb_ref, omlb_ref,
                 s0f_ref, s0b_ref, of_ref, ob_ref, sf_ref, sb_ref, st_ref):
    j = pl.program_id(2)

    @pl.when(j == 0)
    def _():
        st_ref[0] = s0f_ref[0, 0]
        st_ref[1] = s0b_ref[0, 0]

    o, st = _hgrn_direction(qf_ref[0], ff_ref[0], vf_ref[0], loglb_ref[0:1, :], omlb_ref[0:1, :],
                            st_ref[0], False)
    of_ref[0] = o
    st_ref[0] = st
    o, st = _hgrn_direction(qb_ref[0], fb_ref[0], vb_ref[0], loglb_ref[1:2, :], omlb_ref[1:2, :],
                            st_ref[1], True)
    ob_ref[0] = o
    st_ref[1] = st

    @pl.when(j == pl.num_programs(2) - 1)
    def _():
        sf_ref[0, 0] = st_ref[0]
        sb_ref[0, 0] = st_ref[1]


def _hgrn_call(z, log_lb, one_m_lb, s0f, s0b):
    b, l, _ = z.shape
    width = log_lb.shape[1]
    heads = width // HEAD_DIM
    c = min(HGRN_CHUNK, l)
    n = l // c
    col = lambda part: (lambda bi, h, j: (bi, j, part * heads + h))
    col_rev = lambda part: (lambda bi, h, j: (bi, n - 1 - j, part * heads + h))
    tok = lambda imap: pl.BlockSpec((1, c, HEAD_DIM), imap)
    lb_spec = pl.BlockSpec((2, HEAD_DIM), lambda bi, h, j: (0, h))
    st_spec = pl.BlockSpec((1, 1, HEAD_DIM, HEAD_DIM), lambda bi, h, j: (bi, h, 0, 0))
    st_shape = jax.ShapeDtypeStruct((b, heads, HEAD_DIM, HEAD_DIM), F32)
    return pl.pallas_call(
        _hgrn_kernel,
        out_shape=(jax.ShapeDtypeStruct((b, l, width), F32),
                   jax.ShapeDtypeStruct((b, l, width), F32), st_shape, st_shape),
        grid=(b, heads, n),
        in_specs=[tok(col(0)), tok(col(1)), tok(col(3)),
                  tok(col_rev(0)), tok(col_rev(2)), tok(col_rev(3)),
                  lb_spec, lb_spec, st_spec, st_spec],
        out_specs=(tok(lambda bi, h, j: (bi, j, h)), tok(lambda bi, h, j: (bi, n - 1 - j, h)),
                   st_spec, st_spec),
        scratch_shapes=[pltpu.VMEM((2, HEAD_DIM, HEAD_DIM), F32)],
        compiler_params=_params("parallel", "parallel", "arbitrary"),
        name="hgrn2_scan",
    )(z, z, z, z, z, z, log_lb, one_m_lb, s0f, s0b)


def _split_bf16(v):
    hi = v.astype(BF16)
    lo = (v - hi.astype(F32)).astype(BF16)
    return hi, lo


def _pool_project(m, v, g, wp_ref, bp_ref, ps_ref, y_ref):
    cols = slice(g * POOL_GROUP_DIM, (g + 1) * POOL_GROUP_DIM)
    y = jnp.dot((m - v).astype(BF16), wp_ref[g], preferred_element_type=F32)
    y_ref[0, :, cols] = ((y + bp_ref[:, cols]) * ps_ref[:, cols]).astype(y_ref.dtype)


def _pool2d_kernel(prev_ref, cur_ref, next_ref, wp_ref, bp_ref, ps_ref, y_ref, cs_ref, *, rows):
    i = pl.program_id(1)
    tile = cur_ref.shape[1]
    halo = prev_ref.shape[1]
    ext = halo + tile + halo
    first_row = i * POOL_TILE_ROWS - POOL_HALO_ROWS

    tok = lax.broadcasted_iota(jnp.int32, (tile, POOL_GROUP_DIM), 0)
    grow = tok // GRID_W + i * POOL_TILE_ROWS
    gcol = tok % GRID_W
    eo = lax.broadcasted_iota(jnp.int32, (COL_BLOCK, COL_BLOCK), 0)
    ei = lax.broadcasted_iota(jnp.int32, (COL_BLOCK, COL_BLOCK), 1)
    same_row = (eo // GRID_W) == (ei // GRID_W)
    dcol = ei % GRID_W - eo % GRID_W
    erow = lax.broadcasted_iota(jnp.int32, (halo, POOL_GROUP_DIM), 0) // GRID_W

    for g, w in enumerate(POOL_WINDOWS):
        cols = slice(g * POOL_GROUP_DIM, (g + 1) * POOL_GROUP_DIM)
        back = w // 2
        band = jnp.where(same_row & (dcol >= -back) & (dcol < w - back), 1.0, 0.0).astype(BF16)
        pieces = (
            jnp.where(erow + first_row >= 0, prev_ref[0, :, cols], 0.0),
            cur_ref[0, :, cols],
            jnp.where(erow + first_row + POOL_HALO_ROWS + POOL_TILE_ROWS < rows,
                      next_ref[0, :, cols], 0.0),
        )
        off = 0
        for piece in pieces:
            for s in range(0, piece.shape[0], COL_BLOCK):
                hi, lo = _split_bf16(piece[s:s + COL_BLOCK])
                cs_ref[off + s:off + s + COL_BLOCK, :] = (
                    jnp.dot(band, hi, preferred_element_type=F32)
                    + jnp.dot(band, lo, preferred_element_type=F32))
            off += piece.shape[0]
        acc = cs_ref[...]
        span = 1
        while span < w:
            n_tok = ext - (2 * span - 1) * GRID_W
            acc = acc[:n_tok] + acc[span * GRID_W:span * GRID_W + n_tok]
            span *= 2
        start = (POOL_HALO_ROWS - back) * GRID_W
        total = acc[start:start + tile]
        cnt_r = jnp.minimum(grow + (w - back), rows) - jnp.maximum(grow - back, 0)
        cnt_c = jnp.minimum(gcol + (w - back), GRID_W) - jnp.maximum(gcol - back, 0)
        m = total / (cnt_r * cnt_c).astype(F32)
        _pool_project(m, cur_ref[0, :, cols], g, wp_ref, bp_ref, ps_ref, y_ref)


def _pool2d_call(z, w_pool, b_pool, pool_scale, rows):
    b, l, n = z.shape
    width = w_pool.shape[0] * POOL_GROUP_DIM
    p_blk = (n - width) // width
    tile = POOL_TILE_ROWS * GRID_W
    halo = POOL_HALO_ROWS * GRID_W
    n_halo = l // halo
    per = tile // halo
    vec = pl.BlockSpec((1, width), lambda bi, i: (0, 0))
    return pl.pallas_call(
        functools.partial(_pool2d_kernel, rows=rows),
        out_shape=jax.ShapeDtypeStruct((b, l, width), BF16),
        grid=(b, l // tile),
        in_specs=[
            pl.BlockSpec((1, halo, width), lambda bi, i: (bi, jnp.maximum(i * per - 1, 0), p_blk)),
            pl.BlockSpec((1, tile, width), lambda bi, i: (bi, i, p_blk)),
            pl.BlockSpec((1, halo, width),
                         lambda bi, i: (bi, jnp.minimum((i + 1) * per, n_halo - 1), p_blk)),
            pl.BlockSpec(w_pool.shape, lambda bi, i: (0, 0, 0)),
            vec, vec,
        ],
        out_specs=pl.BlockSpec((1, tile, width), lambda bi, i: (bi, i, 0)),
        scratch_shapes=[pltpu.VMEM((halo + tile + halo, POOL_GROUP_DIM), F32)],
        compiler_params=_params("parallel", "parallel"),
        name="pool_mixer_2d",
    )(z, z, z, w_pool, b_pool, pool_scale)


def _pool1d_kernel(p_ref, wp_ref, bp_ref, ps_ref, y_ref):
    l = p_ref.shape[1]
    to = lax.broadcasted_iota(jnp.int32, (l, l), 0)
    dt = lax.broadcasted_iota(jnp.int32, (l, l), 1) - to
    pos = lax.broadcasted_iota(jnp.int32, (l, POOL_GROUP_DIM), 0)
    for g, w in enumerate(POOL_WINDOWS):
        cols = slice(g * POOL_GROUP_DIM, (g + 1) * POOL_GROUP_DIM)
        back = w // 2
        band = jnp.where((dt >= -back) & (dt < w - back), 1.0, 0.0).astype(BF16)
        v = p_ref[0, :, cols]
        hi, lo = _split_bf16(v)
        total = (jnp.dot(band, hi, preferred_element_type=F32)
                 + jnp.dot(band, lo, preferred_element_type=F32))
        cnt = jnp.minimum(pos + (w - back), l) - jnp.maximum(pos - back, 0)
        _pool_project(total / cnt.astype(F32), v, g, wp_ref, bp_ref, ps_ref, y_ref)


def _pool1d_call(z, w_pool, b_pool, pool_scale):
    b, l, n = z.shape
    width = w_pool.shape[0] * POOL_GROUP_DIM
    p_blk = (n - width) // width
    vec = pl.BlockSpec((1, width), lambda bi: (0, 0))
    return pl.pallas_call(
        _pool1d_kernel,
        out_shape=jax.ShapeDtypeStruct((b, l, width), BF16),
        grid=(b,),
        in_specs=[pl.BlockSpec((1, l, width), lambda bi: (bi, 0, p_blk)),
                  pl.BlockSpec(w_pool.shape, lambda bi: (0, 0, 0)), vec, vec],
        out_specs=pl.BlockSpec((1, l, width), lambda bi: (bi, 0, 0)),
        compiler_params=_params("parallel"),
        name="pool_mixer_1d",
    )(z, w_pool, b_pool, pool_scale)


def _outproj_kernel(of_ref, ob_ref, gate_ref, y_ref, x_ref, mod_ref, g_ref, hg_ref, w_ref, o_ref):
    width = of_ref.shape[2]
    o = of_ref[0] + ob_ref[0]
    heads = []
    for h in range(width // HEAD_DIM):
        heads.append(_rms(o[:, h * HEAD_DIM:(h + 1) * HEAD_DIM]) * hg_ref[...])
    gate = gate_ref[0]
    og = (jnp.concatenate(heads, axis=-1) * (gate * _sigmoid(gate))).astype(BF16)
    mix = jnp.dot(og, w_ref[:width, :], preferred_element_type=F32)
    mix += jnp.dot(y_ref[0], w_ref[width:, :], preferred_element_type=F32)
    r = _rms(mix) * g_ref[pl.ds(3, 1), :]
    o_ref[0] = x_ref[0] + mod_ref[0, pl.ds(5, 1), :] * r


def _outproj_call(o_fw, o_bw, z, y, x, mod, gains, hgrn_gain, w):
    b, l, d = x.shape
    width = o_fw.shape[2]
    g_blk = 4
    tm = min(256, l)
    tokw = lambda imap: pl.BlockSpec((1, tm, width), imap)
    row = lambda bi, i: (bi, i, 0)
    return pl.pallas_call(
        _outproj_kernel,
        out_shape=jax.ShapeDtypeStruct((b, l, d), F32),
        grid=(b, l // tm),
        in_specs=[
            tokw(row), tokw(row), tokw(lambda bi, i: (bi, i, g_blk)), tokw(row),
            pl.BlockSpec((1, tm, d), row),
            pl.BlockSpec((1, N_MOD, d), lambda bi, i: (bi, 0, 0)),
            pl.BlockSpec(gains.shape, lambda bi, i: (0, 0)),
            pl.BlockSpec((1, HEAD_DIM), lambda bi, i: (0, 0)),
            pl.BlockSpec(w.shape, lambda bi, i: (0, 0)),
        ],
        out_specs=pl.BlockSpec((1, tm, d), row),
        compiler_params=_params("parallel", "parallel"),
        name="mixer_out_proj",
    )(o_fw, o_bw, z, y, x, mod, gains, hgrn_gain, w)


def kernel(x, c, ctx, c_ctx, w_ada, b_ada, norm_gain, ffn_in, ffn_out, w_in, hgrn_lb, hgrn_gain,
           w_pool, b_pool, pool_scale, w_out):
    bsz, seq, d = x.shape
    depth = w_ada.shape[0]
    rows = seq // GRID_W
    width = hgrn_lb.shape[-1]
    heads = width // HEAD_DIM

    lbp = jax.nn.softmax(hgrn_lb.astype(F32), axis=0)
    lower = jnp.cumsum(lbp, axis=0) - lbp[0]
    pos = lower > 0
    log_lb = jnp.where(pos, jnp.log(jnp.where(pos, lower, 1.0)), LOG_ZERO)
    one_m_lb = 1.0 - lower

    cvec = jnp.zeros((MOD_ROWS, d), F32).at[:bsz].set(c).at[bsz].set(c_ctx)
    mod_all = _mod_call(cvec, w_ada, b_ada)

    ffn_in_b = ffn_in.astype(BF16)
    ffn_out_b = ffn_out.astype(BF16)
    w_in_b = w_in.astype(BF16)
    w_out_b = w_out.astype(BF16)
    w_pool_b = w_pool.astype(BF16)

    s0 = jnp.zeros((bsz, heads, HEAD_DIM, HEAD_DIM), F32)
    for l in range(depth):
        last = l == depth - 1
        gains = norm_gain[l]
        mod_x = mod_all[l, :bsz].reshape(bsz, N_MOD, d)
        mod_c = jnp.broadcast_to(mod_all[l, bsz].reshape(1, N_MOD, d), (bsz, N_MOD, d))
        hg = hgrn_gain[l].reshape(1, HEAD_DIM)
        bp = b_pool[l].reshape(1, -1)
        ps = pool_scale[l].reshape(1, -1)

        x = _ffn_call(x, mod_x, gains, ffn_in_b[l, 0], ffn_out_b[l, 0], 0)
        ctx = _ffn_call(ctx, mod_c, gains, ffn_in_b[l, 0], ffn_out_b[l, 0], 0)

        z_x = _inproj_call(x, mod_x, gains, w_in_b[l])
        z_c = _inproj_call(ctx, mod_c, gains, w_in_b[l])
        oc_fw, oc_bw, s_fw, s_bw = _hgrn_call(z_c, log_lb[l], one_m_lb[l], s0, s0)
        ox_fw, ox_bw, _, _ = _hgrn_call(z_x, log_lb[l], one_m_lb[l], s_fw, s_bw)
        y_x = _pool2d_call(z_x, w_pool_b[l], bp, ps, rows)
        x = _outproj_call(ox_fw, ox_bw, z_x, y_x, x, mod_x, gains, hg, w_out_b[l])
        if not last:
            y_c = _pool1d_call(z_c, w_pool_b[l], bp, ps)
            ctx = _outproj_call(oc_fw, oc_bw, z_c, y_c, ctx, mod_c, gains, hg, w_out_b[l])
            ctx = _ffn_call(ctx, mod_c, gains, ffn_in_b[l, 1], ffn_out_b[l, 1], 2)

        x = _ffn_call(x, mod_x, gains, ffn_in_b[l, 1], ffn_out_b[l, 1], 2)
    return x
```

```python
import functools

import jax
import jax.numpy as jnp
from jax import lax
from jax.experimental import pallas as pl
from jax.experimental.pallas import tpu as pltpu

F32 = jnp.float32
BF16 = jnp.bfloat16

HEAD_DIM = 128
GRID_W = 64
POOL_WINDOWS = (2, 4, 8, 16)
POOL_GROUP_DIM = 256
N_MOD = 9
EPS = 1e-6
LOG_ZERO = -1e30
LOG2_E = 1.4426950408889634
V7X_VMEM_LIMIT_BYTES = 56 * 1024 * 1024
MOD_ROWS = 8
HGRN_CHUNK = 256
POOL_TILE_ROWS = 16
POOL_HALO_ROWS = 8
COL_BLOCK = 256
LANES = 128
NORM_SUM_ROWS = 64
ROW_CHUNK_UNROLL = 4

def _params(*semantics):
    return pltpu.CompilerParams(dimension_semantics=semantics,
                                vmem_limit_bytes=V7X_VMEM_LIMIT_BYTES)


def _sigmoid(x):
    return 1.0 / (1.0 + jnp.exp(-x))


def _rms(x):
    return x * lax.rsqrt(jnp.mean(x * x, axis=-1, keepdims=True) + EPS)


def _split_bf16(v):
    hi = v.astype(BF16)
    lo = (v - hi.astype(F32)).astype(BF16)
    return hi, lo


def _mod_kernel(c_ref, w_ref, b_ref, o_ref):
    c = c_ref[...]
    s = (c * _sigmoid(c)).astype(BF16)
    o_ref[0] = jnp.dot(s, w_ref[0].astype(BF16), preferred_element_type=F32) + b_ref[0]


def _mod_call(cvec, w_ada, b_ada, tn=1024):
    depth, d, n = w_ada.shape
    return pl.pallas_call(
        _mod_kernel,
        out_shape=jax.ShapeDtypeStruct((depth, MOD_ROWS, n), F32),
        grid=(depth, n // tn),
        in_specs=[
            pl.BlockSpec((MOD_ROWS, d), lambda l, i: (0, 0)),
            pl.BlockSpec((1, d, tn), lambda l, i: (l, 0, i)),
            pl.BlockSpec((1, 1, tn), lambda l, i: (l, 0, i)),
        ],
        out_specs=pl.BlockSpec((1, MOD_ROWS, tn), lambda l, i: (l, 0, i)),
        compiler_params=_params("parallel", "parallel"),
        name="adaln_mod",
    )(cvec, w_ada, b_ada.reshape(depth, 1, n))


def _for_row_chunks(n_rows, chunk, body):
    chunk = min(chunk, n_rows)

    def step(r, carry):
        body(pl.ds(pl.multiple_of(r * chunk, chunk), chunk))
        return carry

    lax.fori_loop(0, n_rows // chunk, step, 0, unroll=ROW_CHUNK_UNROLL)


def _lane_blocks(d):
    return [slice(s, s + LANES) for s in range(0, d, LANES)]


def _row_scales(src_ref, rs_ref):
    n_rows, d = src_ref.shape[1:]

    def body(rows):
        acc = None
        for cols in _lane_blocks(d):
            xb = src_ref[0, rows, cols]
            acc = xb * xb if acc is None else acc + xb * xb
        rs_ref[rows, :] = acc

    _for_row_chunks(n_rows, NORM_SUM_ROWS, body)
    ss = jnp.sum(rs_ref[...], axis=-1, keepdims=True)
    rs_ref[...] = jnp.broadcast_to(lax.rsqrt(ss * (1.0 / d) + EPS), rs_ref.shape)


def _modulated_norm(x_ref, rs_ref, gain, shift, scale, h_ref, zero_ref=None):
    n_rows, d = h_ref.shape
    _row_scales(x_ref, rs_ref)
    for cols in _lane_blocks(d):
        mul = gain[:, cols] * (1.0 + scale[:, cols])
        y = x_ref[0, :, cols] * rs_ref[...] * mul + shift[:, cols]
        h_ref[:, cols] = y.astype(BF16)
        if zero_ref is not None:
            zero_ref[0, :, cols] = jnp.zeros((n_rows, LANES), F32)


def _ffn_kernel(x_ref, mod_ref, g_ref, wg_ref, wu_ref, wo_ref, o_ref, h_ref, rs_ref, *, j):
    f = pl.program_id(2)
    tm, d = h_ref.shape

    @pl.when(f == 0)
    def _():
        _modulated_norm(x_ref, rs_ref, g_ref[pl.ds(2 * j, 1), :], mod_ref[0, pl.ds(3 * j, 1), :],
                        mod_ref[0, pl.ds(3 * j + 1, 1), :], h_ref, zero_ref=o_ref)

    h = h_ref[...]
    gate = jnp.dot(h, wg_ref[...], preferred_element_type=F32)
    up = jnp.dot(h, wu_ref[...], preferred_element_type=F32)
    a = (gate * _sigmoid(gate) * up).astype(BF16)
    o_ref[0] += jnp.dot(a, wo_ref[...], preferred_element_type=F32)

    @pl.when(f == pl.num_programs(2) - 1)
    def _():
        _row_scales(o_ref, rs_ref)
        gain = g_ref[pl.ds(2 * j + 1, 1), :]
        half_gate = 0.5 * mod_ref[0, pl.ds(3 * j + 2, 1), :]
        for cols in _lane_blocks(d):
            mul = half_gate[:, cols] * gain[:, cols]
            o_ref[0, :, cols] = x_ref[0, :, cols] + o_ref[0, :, cols] * rs_ref[...] * mul


def _ffn_call(x, mod, gains, w_in, w_out, j, tf=512):
    b, l, d = x.shape
    d_ff = w_out.shape[0]
    tm = min(1024, l)
    nf = d_ff // tf
    return pl.pallas_call(
        functools.partial(_ffn_kernel, j=j),
        out_shape=jax.ShapeDtypeStruct((b, l, d), F32),
        grid=(b, l // tm, nf),
        in_specs=[
            pl.BlockSpec((1, tm, d), lambda bi, i, f: (bi, i, 0)),
            pl.BlockSpec((1, N_MOD, d), lambda bi, i, f: (bi, 0, 0)),
            pl.BlockSpec(gains.shape, lambda bi, i, f: (0, 0)),
            pl.BlockSpec((d, tf), lambda bi, i, f: (0, f)),
            pl.BlockSpec((d, tf), lambda bi, i, f: (0, f + nf)),
            pl.BlockSpec((tf, d), lambda bi, i, f: (f, 0)),
        ],
        out_specs=pl.BlockSpec((1, tm, d), lambda bi, i, f: (bi, i, 0)),
        scratch_shapes=[pltpu.VMEM((tm, d), BF16), pltpu.VMEM((tm, LANES), F32)],
        compiler_params=_params("parallel", "parallel", "arbitrary"),
        name="ffn_sublayer",
    )(x, mod, gains, w_in, w_in, w_out)


def _inproj_kernel(x_ref, mod_ref, g_ref, w_ref, z_ref, h_ref, rs_ref):
    @pl.when(pl.program_id(2) == 0)
    def _():
        _modulated_norm(x_ref, rs_ref, g_ref[pl.ds(2, 1), :], mod_ref[0, pl.ds(3, 1), :],
                        mod_ref[0, pl.ds(4, 1), :], h_ref)

    z_ref[0] = jnp.dot(h_ref[...], w_ref[...], preferred_element_type=F32)


def _inproj_call(x, mod, gains, w, tn=1024):
    b, l, d = x.shape
    n = w.shape[1]
    tm = min(1024, l)
    return pl.pallas_call(
        _inproj_kernel,
        out_shape=jax.ShapeDtypeStruct((b, l, n), F32),
        grid=(b, l // tm, n // tn),
        in_specs=[
            pl.BlockSpec((1, tm, d), lambda bi, i, k: (bi, i, 0)),
            pl.BlockSpec((1, N_MOD, d), lambda bi, i, k: (bi, 0, 0)),
            pl.BlockSpec(gains.shape, lambda bi, i, k: (0, 0)),
            pl.BlockSpec((d, tn), lambda bi, i, k: (0, k)),
        ],
        out_specs=pl.BlockSpec((1, tm, tn), lambda bi, i, k: (bi, i, k)),
        scratch_shapes=[pltpu.VMEM((tm, d), BF16), pltpu.VMEM((tm, LANES), F32)],
        compiler_params=_params("parallel", "parallel", "arbitrary"),
        name="mixer_in_proj",
    )(x, mod, gains, w)


def _neg_abs(d):
    bits = lax.bitcast_convert_type(d, jnp.uint32) | jnp.uint32(0x80000000)
    return lax.bitcast_convert_type(bits, F32)


def _hgrn_gates(q_raw, z, log_lb, one_m_lb):
    e = jnp.exp(-jnp.abs(z))
    ope = 1.0 + e
    log_sig = jnp.minimum(z, 0.0) - jnp.log(ope)
    other = log_sig - z + log_lb
    log_f = jnp.maximum(log_sig, other) + jnp.log(1.0 + jnp.exp(_neg_abs(z - log_lb)))
    k = one_m_lb * (jnp.where(z > 0, e, 1.0) / ope)
    q = q_raw * _sigmoid(q_raw)
    return q, k, log_f


def _mid_rows(b_ref, d, c, half):
    if half >= 4:
        blk = 2 * half
        return jnp.concatenate(
            [jnp.broadcast_to(b_ref[d, pl.ds(s + half - 1, 1), :], (blk, HEAD_DIM))
             for s in range(0, c, blk)], axis=0)
    sub = lax.broadcasted_iota(jnp.int32, (8, HEAD_DIM), 0)
    vregs = []
    for s in range(0, c, 8):
        lo = jnp.broadcast_to(b_ref[d, pl.ds(s + 1, 1), :], (8, HEAD_DIM))
        hi = jnp.broadcast_to(b_ref[d, pl.ds(s + 5, 1), :], (8, HEAD_DIM))
        vregs.append(jnp.where(sub < 4, lo, hi))
    return jnp.concatenate(vregs, axis=0)


def _pick_halves(even_src, odd_src, half, odd_mask):
    c = even_src.shape[0]
    if half < 8:
        return jnp.where(odd_mask, odd_src, even_src)
    shp = (c // (2 * half), 2, half, HEAD_DIM)
    both = jnp.stack([even_src.reshape(shp)[:, 0], odd_src.reshape(shp)[:, 1]], axis=1)
    return both.reshape(c, HEAD_DIM)


def _hgrn_direction(q, k, v, g, b_ref, code_ref, d, st, reverse):
    c = q.shape[0]
    hc = c // 2
    b = b_ref[d]
    beta = b - g if reverse else b
    row = lax.broadcasted_iota(jnp.int32, (c, HEAD_DIM), 0)
    first = jnp.zeros((hc, hc), F32)
    second = jnp.zeros((hc, hc), F32)
    half = 1
    while half < hc:
        odd = (row & half) != 0 if half < 8 else None
        if half == 1:
            u = jnp.where(odd, 0.0, g) if reverse else jnp.where(odd, g, 0.0)
        else:
            u = _neg_abs(beta - _mid_rows(b_ref, d, c, half))
        src = _pick_halves(q, k, half, odd) if reverse else _pick_halves(k, q, half, odd)
        r = (src * jnp.exp2(u)).astype(BF16)
        a = lax.dot_general(r, r, (((1,), (1,)), ((), ())), preferred_element_type=F32)
        first = jnp.where(code_ref[d] >= half, a[:hc, :hc], first)
        second = jnp.where(code_ref[d] >= half, a[hc:, hc:], second)
        half *= 2
    u = _neg_abs(beta - _mid_rows(b_ref, d, c, hc))
    src = _pick_halves(q, k, hc, None) if reverse else _pick_halves(k, q, hc, None)
    r = (src * jnp.exp2(u)).astype(BF16)
    a = lax.dot_general(r, r, (((1,), (1,)), ((), ())), preferred_element_type=F32)

    tot = b_ref[d, pl.ds(c - 1, 1), :]
    q_in, k_out = (tot - beta, beta) if reverse else (b, tot - b)
    q_state = (q * jnp.exp2(q_in)).astype(BF16)
    k_state = (k * jnp.exp2(k_out)).astype(BF16)
    vb = v.astype(BF16)
    if reverse:
        top = jnp.concatenate([first, a[:hc, hc:]], axis=1).astype(BF16)
        o = jnp.concatenate([jnp.dot(top, vb, preferred_element_type=F32),
                             jnp.dot(second.astype(BF16), vb[hc:], preferred_element_type=F32)], axis=0)
    else:
        bottom = jnp.concatenate([a[hc:, :hc], second], axis=1).astype(BF16)
        o = jnp.concatenate([jnp.dot(first.astype(BF16), vb[:hc], preferred_element_type=F32),
                             jnp.dot(bottom, vb, preferred_element_type=F32)], axis=0)
    o += lax.dot_general(q_state, st.astype(BF16), (((1,), (1,)), ((), ())),
                         preferred_element_type=F32)
    o += jnp.sum(q * k, axis=-1, keepdims=True) * v
    st_new = st * jnp.exp2(tot) + lax.dot_general(
        vb, k_state, (((0,), (0,)), ((), ())), preferred_element_type=F32)
    return o, st_new


def _hgrn_kernel(qf_ref, ff_ref, vf_ref, qb_ref, fb_ref, vb_ref, loglb_ref, omlb_ref,
                 s0f_ref, s0b_ref, of_ref, ob_ref, sf_ref, sb_ref, st_ref, b_ref, code_ref, tri_ref):
    j = pl.program_id(2)
    c = qf_ref.shape[1]

    @pl.when(j == 0)
    def _():
        st_ref[0] = s0f_ref[0, 0]
        st_ref[1] = s0b_ref[0, 0]
        ti = lax.broadcasted_iota(jnp.int32, (c, c), 0)
        si = lax.broadcasted_iota(jnp.int32, (c, c), 1)
        tri_ref[...] = jnp.where(ti >= si, 1.0, 0.0).astype(BF16)
        ti = lax.broadcasted_iota(jnp.int32, (c // 2, c // 2), 0)
        si = lax.broadcasted_iota(jnp.int32, (c // 2, c // 2), 1)
        level = ti ^ si
        code_ref[0] = jnp.where(ti > si, level, 0)
        code_ref[1] = jnp.where(ti < si, level, 0)

    qf, kf, gf = _hgrn_gates(qf_ref[0], ff_ref[0], loglb_ref[0:1, :], omlb_ref[0:1, :])
    qb, kb, gb = _hgrn_gates(qb_ref[0], fb_ref[0], loglb_ref[1:2, :], omlb_ref[1:2, :])
    gf = gf * LOG2_E
    gb = gb * LOG2_E

    g2 = jnp.concatenate([gf, gb], axis=1)
    hi, lo = _split_bf16(g2)
    b2 = (jnp.dot(tri_ref[...], hi, preferred_element_type=F32)
          + jnp.dot(tri_ref[...], lo, preferred_element_type=F32))
    b_ref[0] = b2[:, :HEAD_DIM]
    b_ref[1] = b2[:, HEAD_DIM:]

    o, st = _hgrn_direction(qf, kf, vf_ref[0], gf, b_ref, code_ref, 0, st_ref[0], False)
    of_ref[0] = o
    st_ref[0] = st
    o, st = _hgrn_direction(qb, kb, vb_ref[0], gb, b_ref, code_ref, 1, st_ref[1], True)
    ob_ref[0] = o
    st_ref[1] = st

    @pl.when(j == pl.num_programs(2) - 1)
    def _():
        sf_ref[0, 0] = st_ref[0]
        sb_ref[0, 0] = st_ref[1]


def _hgrn_call(z, log_lb, one_m_lb, s0f, s0b):
    b, l, _ = z.shape
    width = log_lb.shape[1]
    heads = width // HEAD_DIM
    c = min(HGRN_CHUNK, l)
    n = l // c
    col = lambda part: (lambda bi, h, j: (bi, j, part * heads + h))
    col_rev = lambda part: (lambda bi, h, j: (bi, n - 1 - j, part * heads + h))
    tok = lambda imap: pl.BlockSpec((1, c, HEAD_DIM), imap)
    lb_spec = pl.BlockSpec((2, HEAD_DIM), lambda bi, h, j: (0, h))
    st_spec = pl.BlockSpec((1, 1, HEAD_DIM, HEAD_DIM), lambda bi, h, j: (bi, h, 0, 0))
    st_shape = jax.ShapeDtypeStruct((b, heads, HEAD_DIM, HEAD_DIM), F32)
    return pl.pallas_call(
        _hgrn_kernel,
        out_shape=(jax.ShapeDtypeStruct((b, l, width), F32),
                   jax.ShapeDtypeStruct((b, l, width), F32), st_shape, st_shape),
        grid=(b, heads, n),
        in_specs=[tok(col(0)), tok(col(1)), tok(col(3)),
                  tok(col_rev(0)), tok(col_rev(2)), tok(col_rev(3)),
                  lb_spec, lb_spec, st_spec, st_spec],
        out_specs=(tok(lambda bi, h, j: (bi, j, h)), tok(lambda bi, h, j: (bi, n - 1 - j, h)),
                   st_spec, st_spec),
        scratch_shapes=[pltpu.VMEM((2, HEAD_DIM, HEAD_DIM), F32),
                        pltpu.VMEM((2, c, HEAD_DIM), F32),
                        pltpu.VMEM((2, c // 2, c // 2), jnp.int32),
                        pltpu.VMEM((c, c), BF16)],
        compiler_params=_params("parallel", "parallel", "arbitrary"),
        name="hgrn2_scan",
    )(z, z, z, z, z, z, log_lb, one_m_lb, s0f, s0b)


def _pool_project(m, v, g, wp_ref, bp_ref, ps_ref, y_ref):
    cols = slice(g * POOL_GROUP_DIM, (g + 1) * POOL_GROUP_DIM)
    y = jnp.dot((m - v).astype(BF16), wp_ref[g], preferred_element_type=F32)
    y_ref[0, :, cols] = ((y + bp_ref[:, cols]) * ps_ref[:, cols]).astype(y_ref.dtype)


def _pool2d_kernel(prev_ref, cur_ref, next_ref, wp_ref, bp_ref, ps_ref, y_ref, cs_ref, *, rows):
    i = pl.program_id(1)
    tile = cur_ref.shape[1]
    halo = prev_ref.shape[1]
    ext = halo + tile + halo
    first_row = i * POOL_TILE_ROWS - POOL_HALO_ROWS

    tok = lax.broadcasted_iota(jnp.int32, (tile, POOL_GROUP_DIM), 0)
    grow = tok // GRID_W + i * POOL_TILE_ROWS
    gcol = tok % GRID_W
    eo = lax.broadcasted_iota(jnp.int32, (COL_BLOCK, COL_BLOCK), 0)
    ei = lax.broadcasted_iota(jnp.int32, (COL_BLOCK, COL_BLOCK), 1)
    same_row = (eo // GRID_W) == (ei // GRID_W)
    dcol = ei % GRID_W - eo % GRID_W
    erow = lax.broadcasted_iota(jnp.int32, (halo, POOL_GROUP_DIM), 0) // GRID_W

    for g, w in enumerate(POOL_WINDOWS):
        cols = slice(g * POOL_GROUP_DIM, (g + 1) * POOL_GROUP_DIM)
        back = w // 2
        band = jnp.where(same_row & (dcol >= -back) & (dcol < w - back), 1.0, 0.0).astype(BF16)
        pieces = (
            jnp.where(erow + first_row >= 0, prev_ref[0, :, cols], 0.0),
            cur_ref[0, :, cols],
            jnp.where(erow + first_row + POOL_HALO_ROWS + POOL_TILE_ROWS < rows,
                      next_ref[0, :, cols], 0.0),
        )
        off = 0
        for piece in pieces:
            for s in range(0, piece.shape[0], COL_BLOCK):
                hi, lo = _split_bf16(piece[s:s + COL_BLOCK])
                cs_ref[off + s:off + s + COL_BLOCK, :] = (
                    jnp.dot(band, hi, preferred_element_type=F32)
                    + jnp.dot(band, lo, preferred_element_type=F32))
            off += piece.shape[0]
        acc = cs_ref[...]
        span = 1
        while span < w:
            n_tok = ext - (2 * span - 1) * GRID_W
            acc = acc[:n_tok] + acc[span * GRID_W:span * GRID_W + n_tok]
            span *= 2
        start = (POOL_HALO_ROWS - back) * GRID_W
        total = acc[start:start + tile]
        cnt_r = jnp.minimum(grow + (w - back), rows) - jnp.maximum(grow - back, 0)
        cnt_c = jnp.minimum(gcol + (w - back), GRID_W) - jnp.maximum(gcol - back, 0)
        m = total / (cnt_r * cnt_c).astype(F32)
        _pool_project(m, cur_ref[0, :, cols], g, wp_ref, bp_ref, ps_ref, y_ref)


def _pool2d_call(z, w_pool, b_pool, pool_scale, rows):
    b, l, n = z.shape
    width = w_pool.shape[0] * POOL_GROUP_DIM
    p_blk = (n - width) // width
    tile = POOL_TILE_ROWS * GRID_W
    halo = POOL_HALO_ROWS * GRID_W
    n_halo = l // halo
    per = tile // halo
    vec = pl.BlockSpec((1, width), lambda bi, i: (0, 0))
    return pl.pallas_call(
        functools.partial(_pool2d_kernel, rows=rows),
        out_shape=jax.ShapeDtypeStruct((b, l, width), BF16),
        grid=(b, l // tile),
        in_specs=[
            pl.BlockSpec((1, halo, width), lambda bi, i: (bi, jnp.maximum(i * per - 1, 0), p_blk)),
            pl.BlockSpec((1, tile, width), lambda bi, i: (bi, i, p_blk)),
            pl.BlockSpec((1, halo, width),
                         lambda bi, i: (bi, jnp.minimum((i + 1) * per, n_halo - 1), p_blk)),
            pl.BlockSpec(w_pool.shape, lambda bi, i: (0, 0, 0)),
            vec, vec,
        ],
        out_specs=pl.BlockSpec((1, tile, width), lambda bi, i: (bi, i, 0)),
        scratch_shapes=[pltpu.VMEM((halo + tile + halo, POOL_GROUP_DIM), F32)],
        compiler_params=_params("parallel", "parallel"),
        name="pool_mixer_2d",
    )(z, z, z, w_pool, b_pool, pool_scale)


def _pool1d_kernel(p_ref, wp_ref, bp_ref, ps_ref, y_ref):
    l = p_ref.shape[1]
    to = lax.broadcasted_iota(jnp.int32, (l, l), 0)
    dt = lax.broadcasted_iota(jnp.int32, (l, l), 1) - to
    pos = lax.broadcasted_iota(jnp.int32, (l, POOL_GROUP_DIM), 0)
    for g, w in enumerate(POOL_WINDOWS):
        cols = slice(g * POOL_GROUP_DIM, (g + 1) * POOL_GROUP_DIM)
        back = w // 2
        band = jnp.where((dt >= -back) & (dt < w - back), 1.0, 0.0).astype(BF16)
        v = p_ref[0, :, cols]
        hi, lo = _split_bf16(v)
        total = (jnp.dot(band, hi, preferred_element_type=F32)
                 + jnp.dot(band, lo, preferred_element_type=F32))
        cnt = jnp.minimum(pos + (w - back), l) - jnp.maximum(pos - back, 0)
        _pool_project(total / cnt.astype(F32), v, g, wp_ref, bp_ref, ps_ref, y_ref)


def _pool1d_call(z, w_pool, b_pool, pool_scale):
    b, l, n = z.shape
    width = w_pool.shape[0] * POOL_GROUP_DIM
    p_blk = (n - width) // width
    vec = pl.BlockSpec((1, width), lambda bi: (0, 0))
    return pl.pallas_call(
        _pool1d_kernel,
        out_shape=jax.ShapeDtypeStruct((b, l, width), BF16),
        grid=(b,),
        in_specs=[pl.BlockSpec((1, l, width), lambda bi: (bi, 0, p_blk)),
                  pl.BlockSpec(w_pool.shape, lambda bi: (0, 0, 0)), vec, vec],
        out_specs=pl.BlockSpec((1, l, width), lambda bi: (bi, 0, 0)),
        compiler_params=_params("parallel"),
        name="pool_mixer_1d",
    )(z, w_pool, b_pool, pool_scale)


def _outproj_kernel(of_ref, ob_ref, gate_ref, y_ref, x_ref, mod_ref, g_ref, hg_ref, w_ref, o_ref):
    width = of_ref.shape[2]
    o = of_ref[0] + ob_ref[0]
    heads = []
    for h in range(width // HEAD_DIM):
        heads.append(_rms(o[:, h * HEAD_DIM:(h + 1) * HEAD_DIM]) * hg_ref[...])
    gate = gate_ref[0]
    og = (jnp.concatenate(heads, axis=-1) * (gate * _sigmoid(gate))).astype(BF16)
    mix = jnp.dot(jnp.concatenate([og, y_ref[0]], axis=-1), w_ref[...], preferred_element_type=F32)
    r = _rms(mix) * g_ref[pl.ds(3, 1), :]
    o_ref[0] = x_ref[0] + mod_ref[0, pl.ds(5, 1), :] * r


def _outproj_call(o_fw, o_bw, z, y, x, mod, gains, hgrn_gain, w):
    b, l, d = x.shape
    width = o_fw.shape[2]
    g_blk = 4
    tm = min(512, l)
    tokw = lambda imap: pl.BlockSpec((1, tm, width), imap)
    row = lambda bi, i: (bi, i, 0)
    return pl.pallas_call(
        _outproj_kernel,
        out_shape=jax.ShapeDtypeStruct((b, l, d), F32),
        grid=(b, l // tm),
        in_specs=[
            tokw(row), tokw(row), tokw(lambda bi, i: (bi, i, g_blk)), tokw(row),
            pl.BlockSpec((1, tm, d), row),
            pl.BlockSpec((1, N_MOD, d), lambda bi, i: (bi, 0, 0)),
            pl.BlockSpec(gains.shape, lambda bi, i: (0, 0)),
            pl.BlockSpec((1, HEAD_DIM), lambda bi, i: (0, 0)),
            pl.BlockSpec(w.shape, lambda bi, i: (0, 0)),
        ],
        out_specs=pl.BlockSpec((1, tm, d), row),
        compiler_params=_params("parallel", "parallel"),
        name="mixer_out_proj",
    )(o_fw, o_bw, z, y, x, mod, gains, hgrn_gain, w)


def kernel(x, c, ctx, c_ctx, w_ada, b_ada, norm_gain, ffn_in, ffn_out, w_in, hgrn_lb, hgrn_gain,
           w_pool, b_pool, pool_scale, w_out):
    bsz, seq, d = x.shape
    depth = w_ada.shape[0]
    rows = seq // GRID_W
    width = hgrn_lb.shape[-1]
    heads = width // HEAD_DIM

    lbp = jax.nn.softmax(hgrn_lb.astype(F32), axis=0)
    lower = jnp.cumsum(lbp, axis=0) - lbp[0]
    pos = lower > 0
    log_lb = jnp.where(pos, jnp.log(jnp.where(pos, lower, 1.0)), LOG_ZERO)
    one_m_lb = 1.0 - lower

    cvec = jnp.zeros((MOD_ROWS, d), F32).at[:bsz].set(c).at[bsz].set(c_ctx)
    mod_all = _mod_call(cvec, w_ada, b_ada)

    s0 = jnp.zeros((bsz, heads, HEAD_DIM, HEAD_DIM), F32)
    for l in range(depth):
        last = l == depth - 1
        gains = norm_gain[l]
        mod_x = mod_all[l, :bsz].reshape(bsz, N_MOD, d)
        mod_c = jnp.broadcast_to(mod_all[l, bsz].reshape(1, N_MOD, d), (bsz, N_MOD, d))
        hg = hgrn_gain[l].reshape(1, HEAD_DIM)
        bp = b_pool[l].reshape(1, -1)
        ps = pool_scale[l].reshape(1, -1)
        ffn1_in, ffn1_out = ffn_in[l, 0].astype(BF16), ffn_out[l, 0].astype(BF16)
        ffn2_in, ffn2_out = ffn_in[l, 1].astype(BF16), ffn_out[l, 1].astype(BF16)
        w_in_b, w_out_b, w_pool_b = w_in[l].astype(BF16), w_out[l].astype(BF16), w_pool[l].astype(BF16)

        x = _ffn_call(x, mod_x, gains, ffn1_in, ffn1_out, 0)
        ctx = _ffn_call(ctx, mod_c, gains, ffn1_in, ffn1_out, 0)

        z_x = _inproj_call(x, mod_x, gains, w_in_b)
        z_c = _inproj_call(ctx, mod_c, gains, w_in_b)
        oc_fw, oc_bw, s_fw, s_bw = _hgrn_call(z_c, log_lb[l], one_m_lb[l], s0, s0)
        ox_fw, ox_bw, _, _ = _hgrn_call(z_x, log_lb[l], one_m_lb[l], s_fw, s_bw)
        y_x = _pool2d_call(z_x, w_pool_b, bp, ps, rows)
        x = _outproj_call(ox_fw, ox_bw, z_x, y_x, x, mod_x, gains, hg, w_out_b)
        if not last:
            y_c = _pool1d_call(z_c, w_pool_b, bp, ps)
            ctx = _outproj_call(oc_fw, oc_bw, z_c, y_c, ctx, mod_c, gains, hg, w_out_b)
            ctx = _ffn_call(ctx, mod_c, gains, ffn2_in, ffn2_out, 2)

        x = _ffn_call(x, mod_x, gains, ffn2_in, ffn2_out, 2)
    return x
```

```python
import functools

import jax
import jax.numpy as jnp
from jax import lax
from jax.experimental import pallas as pl
from jax.experimental.pallas import tpu as pltpu

F32 = jnp.float32
BF16 = jnp.bfloat16

HEAD_DIM = 128
GRID_W = 64
POOL_WINDOWS = (2, 4, 8, 16)
POOL_GROUP_DIM = 256
N_MOD = 9
EPS = 1e-6
LOG_ZERO = -1e30
LOG2_E = 1.4426950408889634
PART_Q, PART_F_FWD, PART_F_BWD, PART_V, PART_GATE, PART_POOL = range(6)
V7X_VMEM_LIMIT_BYTES = 56 * 1024 * 1024
MOD_ROWS = 8
HGRN_CHUNK = 256
POOL_TILE_ROWS = 16
POOL_HALO_ROWS = 8
COL_BLOCK = 256
LANES = 128
EDGE_SUB_ROWS = 256

def _params(*semantics):
    return pltpu.CompilerParams(dimension_semantics=semantics,
                                vmem_limit_bytes=V7X_VMEM_LIMIT_BYTES)


def _sigmoid(x):
    return 1.0 / (1.0 + jnp.exp(-x))


def _neg_abs(d):
    bits = lax.bitcast_convert_type(d, jnp.uint32) | jnp.uint32(0x80000000)
    return lax.bitcast_convert_type(bits, F32)


def _rms(x):
    return x * lax.rsqrt(jnp.mean(x * x, axis=-1, keepdims=True) + EPS)


def _split_bf16(v):
    hi = v.astype(BF16)
    lo = (v - hi.astype(F32)).astype(BF16)
    return hi, lo


def _mod_kernel(c_ref, w_ref, b_ref, o_ref):
    c = c_ref[...]
    s = (c * _sigmoid(c)).astype(BF16)
    o_ref[0] = jnp.dot(s, w_ref[0].astype(BF16), preferred_element_type=F32) + b_ref[0]


def _mod_call(cvec, w_ada, b_ada, tn=1024):
    depth, d, n = w_ada.shape
    return pl.pallas_call(
        _mod_kernel,
        out_shape=jax.ShapeDtypeStruct((depth, MOD_ROWS, n), F32),
        grid=(depth, n // tn),
        in_specs=[
            pl.BlockSpec((MOD_ROWS, d), lambda l, i: (0, 0)),
            pl.BlockSpec((1, d, tn), lambda l, i: (l, 0, i)),
            pl.BlockSpec((1, 1, tn), lambda l, i: (l, 0, i)),
        ],
        out_specs=pl.BlockSpec((1, MOD_ROWS, tn), lambda l, i: (l, 0, i)),
        compiler_params=_params("parallel", "parallel"),
        name="adaln_mod",
    )(cvec, w_ada, b_ada.reshape(depth, 1, n))


def _lane_blocks(d):
    return [slice(s, s + LANES) for s in range(0, d, LANES)]


def _row_tiles(n_rows):
    sub = min(EDGE_SUB_ROWS, n_rows)
    return [slice(r, r + sub) for r in range(0, n_rows, sub)]


def _row_scale(src_ref, rows):
    d = src_ref.shape[2]
    acc = None
    for cols in _lane_blocks(d):
        xb = src_ref[0, rows, cols]
        acc = xb * xb if acc is None else acc + xb * xb
    ss = jnp.sum(acc, axis=-1, keepdims=True)
    return jnp.broadcast_to(lax.rsqrt(ss * (1.0 / d) + EPS), acc.shape)


def _modulated_norm(x_ref, rows, gain, shift, scale, h_ref):
    rs = _row_scale(x_ref, rows)
    for cols in _lane_blocks(x_ref.shape[2]):
        mul = gain[:, cols] * (1.0 + scale[:, cols])
        h_ref[rows, cols] = (x_ref[0, rows, cols] * rs * mul + shift[:, cols]).astype(BF16)


def _ffn_kernel(x_ref, mod_ref, g_ref, wg_ref, wu_ref, wo_ref, o_ref, h_ref, *, j):
    f = pl.program_id(2)
    last = pl.num_programs(2) - 1
    tm, d = h_ref.shape

    def swiglu(rows):
        h = h_ref[rows, :]
        gate = jnp.dot(h, wg_ref[...], preferred_element_type=F32)
        up = jnp.dot(h, wu_ref[...], preferred_element_type=F32)
        a = (gate * _sigmoid(gate) * up).astype(BF16)
        return jnp.dot(a, wo_ref[...], preferred_element_type=F32)

    @pl.when(f == 0)
    def _():
        gain = g_ref[pl.ds(2 * j, 1), :]
        shift = mod_ref[0, pl.ds(3 * j, 1), :]
        scale = mod_ref[0, pl.ds(3 * j + 1, 1), :]
        for rows in _row_tiles(tm):
            _modulated_norm(x_ref, rows, gain, shift, scale, h_ref)
            o_ref[0, rows, :] = swiglu(rows)

    @pl.when((f > 0) & (f < last))
    def _():
        o_ref[0] += swiglu(slice(None))

    @pl.when(f == last)
    def _():
        gain = g_ref[pl.ds(2 * j + 1, 1), :]
        half_gate = 0.5 * mod_ref[0, pl.ds(3 * j + 2, 1), :]
        for rows in _row_tiles(tm):
            o_ref[0, rows, :] += swiglu(rows)
            rs = _row_scale(o_ref, rows)
            for cols in _lane_blocks(d):
                mul = half_gate[:, cols] * gain[:, cols]
                o_ref[0, rows, cols] = x_ref[0, rows, cols] + o_ref[0, rows, cols] * rs * mul


def _ffn_call(x, mod, gains, w_in, w_out, layer, which, tf=512):
    b, l, d = x.shape
    d_ff = w_out.shape[2]
    tm = min(1024, l)
    nf = d_ff // tf
    assert nf >= 2, "the kernel's first and last grid steps are distinct code paths"
    return pl.pallas_call(
        functools.partial(_ffn_kernel, j=2 * which),
        out_shape=jax.ShapeDtypeStruct((b, l, d), F32),
        grid=(b, l // tm, nf),
        in_specs=[
            pl.BlockSpec((1, tm, d), lambda bi, i, f: (bi, i, 0)),
            pl.BlockSpec((1, N_MOD, d), lambda bi, i, f: (bi, 0, 0)),
            pl.BlockSpec(gains.shape, lambda bi, i, f: (0, 0)),
            pl.BlockSpec((None, None, d, tf), lambda bi, i, f: (layer, which, 0, f)),
            pl.BlockSpec((None, None, d, tf), lambda bi, i, f: (layer, which, 0, f + nf)),
            pl.BlockSpec((None, None, tf, d), lambda bi, i, f: (layer, which, f, 0)),
        ],
        out_specs=pl.BlockSpec((1, tm, d), lambda bi, i, f: (bi, i, 0)),
        scratch_shapes=[pltpu.VMEM((tm, d), BF16)],
        compiler_params=_params("parallel", "parallel", "arbitrary"),
        name="ffn_sublayer",
    )(x, mod, gains, w_in, w_in, w_out)


def _decay_gates(z, log2_lb, one_m_lb):
    z2 = z * LOG2_E
    e = jnp.exp2(_neg_abs(z2))
    ope = 1.0 + e
    log2_sig = jnp.minimum(z2, 0.0) - jnp.log2(ope)
    other = log2_sig - z2 + log2_lb
    log2_f = jnp.maximum(log2_sig, other) + jnp.log2(1.0 + jnp.exp2(_neg_abs(z2 - log2_lb)))
    k = jnp.where(z > 0, e * one_m_lb, one_m_lb) / ope
    return k, log2_f


def _inproj_kernel(x_ref, mod_ref, g_ref, w_ref, loglb_ref, omlb_ref, z_ref, zk_ref, h_ref):
    part = pl.program_id(2)
    row_tiles = _row_tiles(h_ref.shape[0])

    def project(rows):
        return jnp.dot(h_ref[rows, :], w_ref[...], preferred_element_type=F32)

    @pl.when(part == PART_Q)
    def _():
        gain, shift, scale = g_ref[pl.ds(2, 1), :], mod_ref[0, pl.ds(3, 1), :], mod_ref[0, pl.ds(4, 1), :]
        for rows in row_tiles:
            _modulated_norm(x_ref, rows, gain, shift, scale, h_ref)
            a = project(rows)
            z_ref[0, rows, :] = a * _sigmoid(a)

    @pl.when(part == PART_GATE)
    def _():
        for rows in row_tiles:
            a = project(rows)
            z_ref[0, rows, :] = a * _sigmoid(a)

    for direction, forget_part in enumerate((PART_F_FWD, PART_F_BWD)):
        @pl.when(part == forget_part)
        def _(direction=direction):
            for rows in row_tiles:
                k, log2_f = _decay_gates(project(rows), loglb_ref[direction:direction + 1, :],
                                         omlb_ref[direction:direction + 1, :])
                z_ref[0, rows, :] = log2_f
                zk_ref[0, rows, :] = k

    @pl.when((part == PART_V) | (part == PART_POOL))
    def _():
        z_ref[0] = project(slice(None))


def _inproj_call(x, mod, gains, w, layer, log2_lb, one_m_lb):
    b, l, d = x.shape
    n = w.shape[2]
    width = log2_lb.shape[1]
    tm = min(1024, l)
    lb_spec = pl.BlockSpec(log2_lb.shape, lambda bi, i, k: (0, 0))
    return pl.pallas_call(
        _inproj_kernel,
        out_shape=(jax.ShapeDtypeStruct((b, l, n), F32),
                   jax.ShapeDtypeStruct((b, l, 2 * width), F32)),
        grid=(b, l // tm, n // width),
        in_specs=[
            pl.BlockSpec((1, tm, d), lambda bi, i, k: (bi, i, 0)),
            pl.BlockSpec((1, N_MOD, d), lambda bi, i, k: (bi, 0, 0)),
            pl.BlockSpec(gains.shape, lambda bi, i, k: (0, 0)),
            pl.BlockSpec((None, d, width), lambda bi, i, k: (layer, 0, k)),
            lb_spec, lb_spec,
        ],
        out_specs=(pl.BlockSpec((1, tm, width), lambda bi, i, k: (bi, i, k)),
                   pl.BlockSpec((1, tm, width),
                                lambda bi, i, k: (bi, i, jnp.clip(k - PART_F_FWD, 0, 1)))),
        scratch_shapes=[pltpu.VMEM((tm, d), BF16)],
        compiler_params=_params("parallel", "parallel", "arbitrary"),
        name="mixer_in_proj",
    )(x, mod, gains, w, log2_lb, one_m_lb)


def _mid_rows(b_ref, d, c, half):
    if half >= 4:
        blk = 2 * half
        return jnp.concatenate(
            [jnp.broadcast_to(b_ref[d, pl.ds(s + half - 1, 1), :], (blk, HEAD_DIM))
             for s in range(0, c, blk)], axis=0)
    sub = lax.broadcasted_iota(jnp.int32, (8, HEAD_DIM), 0)
    vregs = []
    for s in range(0, c, 8):
        lo = jnp.broadcast_to(b_ref[d, pl.ds(s + 1, 1), :], (8, HEAD_DIM))
        hi = jnp.broadcast_to(b_ref[d, pl.ds(s + 5, 1), :], (8, HEAD_DIM))
        vregs.append(jnp.where(sub < 4, lo, hi))
    return jnp.concatenate(vregs, axis=0)


def _pick_halves(even_src, odd_src, half, odd_mask):
    c = even_src.shape[0]
    if half < 8:
        return jnp.where(odd_mask, odd_src, even_src)
    shp = (c // (2 * half), 2, half, HEAD_DIM)
    both = jnp.stack([even_src.reshape(shp)[:, 0], odd_src.reshape(shp)[:, 1]], axis=1)
    return both.reshape(c, HEAD_DIM)


def _hgrn_direction(q, k, v, g, b_ref, code_ref, d, st, reverse):
    c = q.shape[0]
    hc = c // 2
    b = b_ref[d]
    beta = b - g if reverse else b
    row = lax.broadcasted_iota(jnp.int32, (c, HEAD_DIM), 0)
    first = jnp.zeros((hc, hc), F32)
    second = jnp.zeros((hc, hc), F32)
    half = 1
    while half < hc:
        odd = (row & half) != 0 if half < 8 else None
        if half == 1:
            u = jnp.where(odd, 0.0, g) if reverse else jnp.where(odd, g, 0.0)
        else:
            u = _neg_abs(beta - _mid_rows(b_ref, d, c, half))
        src = _pick_halves(q, k, half, odd) if reverse else _pick_halves(k, q, half, odd)
        r = (src * jnp.exp2(u)).astype(BF16)
        a = lax.dot_general(r, r, (((1,), (1,)), ((), ())), preferred_element_type=F32)
        first = jnp.where(code_ref[d] >= half, a[:hc, :hc], first)
        second = jnp.where(code_ref[d] >= half, a[hc:, hc:], second)
        half *= 2
    u = _neg_abs(beta - _mid_rows(b_ref, d, c, hc))
    src = _pick_halves(q, k, hc, None) if reverse else _pick_halves(k, q, hc, None)
    r = (src * jnp.exp2(u)).astype(BF16)
    a = lax.dot_general(r, r, (((1,), (1,)), ((), ())), preferred_element_type=F32)

    tot = b_ref[d, pl.ds(c - 1, 1), :]
    q_in, k_out = (tot - beta, beta) if reverse else (b, tot - b)
    q_state = (q * jnp.exp2(q_in)).astype(BF16)
    k_state = (k * jnp.exp2(k_out)).astype(BF16)
    vb = v.astype(BF16)
    if reverse:
        top = jnp.concatenate([first, a[:hc, hc:]], axis=1).astype(BF16)
        o = jnp.concatenate([jnp.dot(top, vb, preferred_element_type=F32),
                             jnp.dot(second.astype(BF16), vb[hc:], preferred_element_type=F32)], axis=0)
    else:
        bottom = jnp.concatenate([a[hc:, :hc], second], axis=1).astype(BF16)
        o = jnp.concatenate([jnp.dot(first.astype(BF16), vb[:hc], preferred_element_type=F32),
                             jnp.dot(bottom, vb, preferred_element_type=F32)], axis=0)
    o += lax.dot_general(q_state, st.astype(BF16), (((1,), (1,)), ((), ())),
                         preferred_element_type=F32)
    o += jnp.sum(q * k, axis=-1, keepdims=True) * v
    st_new = st * jnp.exp2(tot) + lax.dot_general(
        vb, k_state, (((0,), (0,)), ((), ())), preferred_element_type=F32)
    return o, st_new


def _hgrn_kernel(qf_ref, gf_ref, kf_ref, vf_ref, qb_ref, gb_ref, kb_ref, vb_ref,
                 s0f_ref, s0b_ref, of_ref, ob_ref, sf_ref, sb_ref, st_ref, b_ref, code_ref, tri_ref):
    j = pl.program_id(2)
    c = qf_ref.shape[1]

    @pl.when(j == 0)
    def _():
        st_ref[0] = s0f_ref[0, 0]
        st_ref[1] = s0b_ref[0, 0]
        ti = lax.broadcasted_iota(jnp.int32, (c, c), 0)
        si = lax.broadcasted_iota(jnp.int32, (c, c), 1)
        tri_ref[...] = jnp.where(ti >= si, 1.0, 0.0).astype(BF16)
        ti = lax.broadcasted_iota(jnp.int32, (c // 2, c // 2), 0)
        si = lax.broadcasted_iota(jnp.int32, (c // 2, c // 2), 1)
        level = ti ^ si
        code_ref[0] = jnp.where(ti > si, level, 0)
        code_ref[1] = jnp.where(ti < si, level, 0)

    gf = gf_ref[0]
    gb = gb_ref[0]
    g2 = jnp.concatenate([gf, gb], axis=1)
    hi, lo = _split_bf16(g2)
    b2 = (jnp.dot(tri_ref[...], hi, preferred_element_type=F32)
          + jnp.dot(tri_ref[...], lo, preferred_element_type=F32))
    b_ref[0] = b2[:, :HEAD_DIM]
    b_ref[1] = b2[:, HEAD_DIM:]

    o, st = _hgrn_direction(qf_ref[0], kf_ref[0], vf_ref[0], gf, b_ref, code_ref, 0, st_ref[0], False)
    of_ref[0] = o
    st_ref[0] = st
    o, st = _hgrn_direction(qb_ref[0], kb_ref[0], vb_ref[0], gb, b_ref, code_ref, 1, st_ref[1], True)
    ob_ref[0] = o
    st_ref[1] = st

    @pl.when(j == pl.num_programs(2) - 1)
    def _():
        sf_ref[0, 0] = st_ref[0]
        sb_ref[0, 0] = st_ref[1]


def _hgrn_call(z, zk, s0f, s0b):
    b, l, _ = z.shape
    width = zk.shape[2] // 2
    heads = width // HEAD_DIM
    c = min(HGRN_CHUNK, l)
    n = l // c
    col = lambda part: (lambda bi, h, j: (bi, j, part * heads + h))
    col_rev = lambda part: (lambda bi, h, j: (bi, n - 1 - j, part * heads + h))
    tok = lambda imap: pl.BlockSpec((1, c, HEAD_DIM), imap)
    st_spec = pl.BlockSpec((1, 1, HEAD_DIM, HEAD_DIM), lambda bi, h, j: (bi, h, 0, 0))
    st_shape = jax.ShapeDtypeStruct((b, heads, HEAD_DIM, HEAD_DIM), F32)
    return pl.pallas_call(
        _hgrn_kernel,
        out_shape=(jax.ShapeDtypeStruct((b, l, width), F32),
                   jax.ShapeDtypeStruct((b, l, width), F32), st_shape, st_shape),
        grid=(b, heads, n),
        in_specs=[tok(col(PART_Q)), tok(col(PART_F_FWD)), tok(col(0)), tok(col(PART_V)),
                  tok(col_rev(PART_Q)), tok(col_rev(PART_F_BWD)), tok(col_rev(1)), tok(col_rev(PART_V)),
                  st_spec, st_spec],
        out_specs=(tok(lambda bi, h, j: (bi, j, h)), tok(lambda bi, h, j: (bi, n - 1 - j, h)),
                   st_spec, st_spec),
        scratch_shapes=[pltpu.VMEM((2, HEAD_DIM, HEAD_DIM), F32),
                        pltpu.VMEM((2, c, HEAD_DIM), F32),
                        pltpu.VMEM((2, c // 2, c // 2), jnp.int32),
                        pltpu.VMEM((c, c), BF16)],
        compiler_params=_params("parallel", "parallel", "arbitrary"),
        name="hgrn2_scan",
    )(z, z, zk, z, z, z, zk, z, s0f, s0b)


def _pool_project(m, v, g, wp_ref, bp_ref, ps_ref, y_ref):
    cols = slice(g * POOL_GROUP_DIM, (g + 1) * POOL_GROUP_DIM)
    y = jnp.dot((m - v).astype(BF16), wp_ref[g], preferred_element_type=F32)
    y_ref[0, :, cols] = ((y + bp_ref[:, cols]) * ps_ref[:, cols]).astype(y_ref.dtype)


def _pool2d_kernel(prev_ref, cur_ref, next_ref, wp_ref, bp_ref, ps_ref, y_ref, cs_ref, *, rows):
    i = pl.program_id(1)
    tile = cur_ref.shape[1]
    halo = prev_ref.shape[1]
    ext = halo + tile + halo
    first_row = i * POOL_TILE_ROWS - POOL_HALO_ROWS

    tok = lax.broadcasted_iota(jnp.int32, (tile, POOL_GROUP_DIM), 0)
    grow = tok // GRID_W + i * POOL_TILE_ROWS
    gcol = tok % GRID_W
    eo = lax.broadcasted_iota(jnp.int32, (COL_BLOCK, COL_BLOCK), 0)
    ei = lax.broadcasted_iota(jnp.int32, (COL_BLOCK, COL_BLOCK), 1)
    same_row = (eo // GRID_W) == (ei // GRID_W)
    dcol = ei % GRID_W - eo % GRID_W
    erow = lax.broadcasted_iota(jnp.int32, (halo, POOL_GROUP_DIM), 0) // GRID_W

    for g, w in enumerate(POOL_WINDOWS):
        cols = slice(g * POOL_GROUP_DIM, (g + 1) * POOL_GROUP_DIM)
        back = w // 2
        band = jnp.where(same_row & (dcol >= -back) & (dcol < w - back), 1.0, 0.0).astype(BF16)
        pieces = (
            jnp.where(erow + first_row >= 0, prev_ref[0, :, cols], 0.0),
            cur_ref[0, :, cols],
            jnp.where(erow + first_row + POOL_HALO_ROWS + POOL_TILE_ROWS < rows,
                      next_ref[0, :, cols], 0.0),
        )
        off = 0
        for piece in pieces:
            for s in range(0, piece.shape[0], COL_BLOCK):
                hi, lo = _split_bf16(piece[s:s + COL_BLOCK])
                cs_ref[off + s:off + s + COL_BLOCK, :] = (
                    jnp.dot(band, hi, preferred_element_type=F32)
                    + jnp.dot(band, lo, preferred_element_type=F32))
            off += piece.shape[0]
        acc = cs_ref[...]
        span = 1
        while span < w:
            n_tok = ext - (2 * span - 1) * GRID_W
            acc = acc[:n_tok] + acc[span * GRID_W:span * GRID_W + n_tok]
            span *= 2
        start = (POOL_HALO_ROWS - back) * GRID_W
        total = acc[start:start + tile]
        cnt_r = jnp.minimum(grow + (w - back), rows) - jnp.maximum(grow - back, 0)
        cnt_c = jnp.minimum(gcol + (w - back), GRID_W) - jnp.maximum(gcol - back, 0)
        m = total / (cnt_r * cnt_c).astype(F32)
        _pool_project(m, cur_ref[0, :, cols], g, wp_ref, bp_ref, ps_ref, y_ref)


def _pool2d_call(z, w_pool, b_pool, pool_scale, rows):
    b, l, n = z.shape
    width = w_pool.shape[0] * POOL_GROUP_DIM
    p_blk = PART_POOL
    tile = POOL_TILE_ROWS * GRID_W
    halo = POOL_HALO_ROWS * GRID_W
    n_halo = l // halo
    per = tile // halo
    vec = pl.BlockSpec((1, width), lambda bi, i: (0, 0))
    return pl.pallas_call(
        functools.partial(_pool2d_kernel, rows=rows),
        out_shape=jax.ShapeDtypeStruct((b, l, width), BF16),
        grid=(b, l // tile),
        in_specs=[
            pl.BlockSpec((1, halo, width), lambda bi, i: (bi, jnp.maximum(i * per - 1, 0), p_blk)),
            pl.BlockSpec((1, tile, width), lambda bi, i: (bi, i, p_blk)),
            pl.BlockSpec((1, halo, width),
                         lambda bi, i: (bi, jnp.minimum((i + 1) * per, n_halo - 1), p_blk)),
            pl.BlockSpec(w_pool.shape, lambda bi, i: (0, 0, 0)),
            vec, vec,
        ],
        out_specs=pl.BlockSpec((1, tile, width), lambda bi, i: (bi, i, 0)),
        scratch_shapes=[pltpu.VMEM((halo + tile + halo, POOL_GROUP_DIM), F32)],
        compiler_params=_params("parallel", "parallel"),
        name="pool_mixer_2d",
    )(z, z, z, w_pool, b_pool, pool_scale)


def _pool1d_kernel(p_ref, wp_ref, bp_ref, ps_ref, y_ref):
    l = p_ref.shape[1]
    to = lax.broadcasted_iota(jnp.int32, (l, l), 0)
    dt = lax.broadcasted_iota(jnp.int32, (l, l), 1) - to
    pos = lax.broadcasted_iota(jnp.int32, (l, POOL_GROUP_DIM), 0)
    for g, w in enumerate(POOL_WINDOWS):
        cols = slice(g * POOL_GROUP_DIM, (g + 1) * POOL_GROUP_DIM)
        back = w // 2
        band = jnp.where((dt >= -back) & (dt < w - back), 1.0, 0.0).astype(BF16)
        v = p_ref[0, :, cols]
        hi, lo = _split_bf16(v)
        total = (jnp.dot(band, hi, preferred_element_type=F32)
                 + jnp.dot(band, lo, preferred_element_type=F32))
        cnt = jnp.minimum(pos + (w - back), l) - jnp.maximum(pos - back, 0)
        _pool_project(total / cnt.astype(F32), v, g, wp_ref, bp_ref, ps_ref, y_ref)


def _pool1d_call(z, w_pool, b_pool, pool_scale):
    b, l, n = z.shape
    width = w_pool.shape[0] * POOL_GROUP_DIM
    p_blk = PART_POOL
    vec = pl.BlockSpec((1, width), lambda bi: (0, 0))
    return pl.pallas_call(
        _pool1d_kernel,
        out_shape=jax.ShapeDtypeStruct((b, l, width), BF16),
        grid=(b,),
        in_specs=[pl.BlockSpec((1, l, width), lambda bi: (bi, 0, p_blk)),
                  pl.BlockSpec(w_pool.shape, lambda bi: (0, 0, 0)), vec, vec],
        out_specs=pl.BlockSpec((1, l, width), lambda bi: (bi, 0, 0)),
        compiler_params=_params("parallel"),
        name="pool_mixer_1d",
    )(z, w_pool, b_pool, pool_scale)


def _outproj_kernel(of_ref, ob_ref, gate_ref, y_ref, x_ref, mod_ref, g_ref, hg_ref, w_ref, o_ref):
    width = of_ref.shape[2]
    o = of_ref[0] + ob_ref[0]
    heads = []
    for h in range(width // HEAD_DIM):
        heads.append(_rms(o[:, h * HEAD_DIM:(h + 1) * HEAD_DIM]) * hg_ref[...])
    og = (jnp.concatenate(heads, axis=-1) * gate_ref[0]).astype(BF16)
    mix = jnp.dot(jnp.concatenate([og, y_ref[0]], axis=-1), w_ref[...], preferred_element_type=F32)
    r = _rms(mix) * g_ref[pl.ds(3, 1), :]
    o_ref[0] = x_ref[0] + mod_ref[0, pl.ds(5, 1), :] * r


def _outproj_call(o_fw, o_bw, z, y, x, mod, gains, hgrn_gain, w, layer):
    b, l, d = x.shape
    width = o_fw.shape[2]
    g_blk = PART_GATE
    tm = min(512, l)
    tokw = lambda imap: pl.BlockSpec((1, tm, width), imap)
    row = lambda bi, i: (bi, i, 0)
    return pl.pallas_call(
        _outproj_kernel,
        out_shape=jax.ShapeDtypeStruct((b, l, d), F32),
        grid=(b, l // tm),
        in_specs=[
            tokw(row), tokw(row), tokw(lambda bi, i: (bi, i, g_blk)), tokw(row),
            pl.BlockSpec((1, tm, d), row),
            pl.BlockSpec((1, N_MOD, d), lambda bi, i: (bi, 0, 0)),
            pl.BlockSpec(gains.shape, lambda bi, i: (0, 0)),
            pl.BlockSpec((1, HEAD_DIM), lambda bi, i: (0, 0)),
            pl.BlockSpec((None,) + w.shape[1:], lambda bi, i: (layer, 0, 0)),
        ],
        out_specs=pl.BlockSpec((1, tm, d), row),
        compiler_params=_params("parallel", "parallel"),
        name="mixer_out_proj",
    )(o_fw, o_bw, z, y, x, mod, gains, hgrn_gain, w)


def kernel(x, c, ctx, c_ctx, w_ada, b_ada, norm_gain, ffn_in, ffn_out, w_in, hgrn_lb, hgrn_gain,
           w_pool, b_pool, pool_scale, w_out):
    bsz, seq, d = x.shape
    depth = w_ada.shape[0]
    rows = seq // GRID_W
    width = hgrn_lb.shape[-1]
    heads = width // HEAD_DIM

    lbp = jax.nn.softmax(hgrn_lb.astype(F32), axis=0)
    lower = jnp.cumsum(lbp, axis=0) - lbp[0]
    pos = lower > 0
    log2_lb = jnp.where(pos, jnp.log(jnp.where(pos, lower, 1.0)), LOG_ZERO) * LOG2_E
    one_m_lb = 1.0 - lower

    cvec = jnp.zeros((MOD_ROWS, d), F32).at[:bsz].set(c).at[bsz].set(c_ctx)
    mod_all = _mod_call(cvec, w_ada, b_ada)

    ffn_in_b, ffn_out_b = ffn_in.astype(BF16), ffn_out.astype(BF16)
    w_in_b, w_out_b, w_pool_b = w_in.astype(BF16), w_out.astype(BF16), w_pool.astype(BF16)

    s0 = jnp.zeros((bsz, heads, HEAD_DIM, HEAD_DIM), F32)
    for l in range(depth):
        last = l == depth - 1
        gains = norm_gain[l]
        mod_x = mod_all[l, :bsz].reshape(bsz, N_MOD, d)
        mod_c = jnp.broadcast_to(mod_all[l, bsz].reshape(1, N_MOD, d), (bsz, N_MOD, d))
        hg = hgrn_gain[l].reshape(1, HEAD_DIM)
        bp = b_pool[l].reshape(1, -1)
        ps = pool_scale[l].reshape(1, -1)

        x = _ffn_call(x, mod_x, gains, ffn_in_b, ffn_out_b, l, 0)
        ctx = _ffn_call(ctx, mod_c, gains, ffn_in_b, ffn_out_b, l, 0)

        z_x, zk_x = _inproj_call(x, mod_x, gains, w_in_b, l, log2_lb[l], one_m_lb[l])
        z_c, zk_c = _inproj_call(ctx, mod_c, gains, w_in_b, l, log2_lb[l], one_m_lb[l])
        oc_fw, oc_bw, s_fw, s_bw = _hgrn_call(z_c, zk_c, s0, s0)
        ox_fw, ox_bw, _, _ = _hgrn_call(z_x, zk_x, s_fw, s_bw)
        y_x = _pool2d_call(z_x, w_pool_b[l], bp, ps, rows)
        x = _outproj_call(ox_fw, ox_bw, z_x, y_x, x, mod_x, gains, hg, w_out_b, l)
        if not last:
            y_c = _pool1d_call(z_c, w_pool_b[l], bp, ps)
            ctx = _outproj_call(oc_fw, oc_bw, z_c, y_c, ctx, mod_c, gains, hg, w_out_b, l)
            ctx = _ffn_call(ctx, mod_c, gains, ffn_in_b, ffn_out_b, l, 1)

        x = _ffn_call(x, mod_x, gains, ffn_in_b, ffn_out_b, l, 1)
    return x
```

```python
import functools

import jax
import jax.numpy as jnp
from jax import lax
from jax.experimental import pallas as pl
from jax.experimental.pallas import tpu as pltpu

F32 = jnp.float32
BF16 = jnp.bfloat16

HEAD_DIM = 128
GRID_W = 64
POOL_WINDOWS = (2, 4, 8, 16)
POOL_GROUP_DIM = 256
N_MOD = 9
EPS = 1e-6
LOG2_E = 1.4426950408889634
PART_Q, PART_F_FWD, PART_F_BWD, PART_V, PART_GATE, PART_POOL = range(6)
V7X_VMEM_LIMIT_BYTES = 56 * 1024 * 1024
MOD_ROWS = 8
HGRN_CHUNK = 256
HGRN_HEADS_PER_STEP = 8
POOL_TILE_ROWS = 16
POOL_HALO_ROWS = 8
COL_BLOCK = 256
LANES = 128
EDGE_SUB_ROWS = 256

def _params(*semantics):
    return pltpu.CompilerParams(dimension_semantics=semantics,
                                vmem_limit_bytes=V7X_VMEM_LIMIT_BYTES)


def _sigmoid(x):
    return 1.0 / (1.0 + jnp.exp(-x))


def _neg_abs(d):
    bits = lax.bitcast_convert_type(d, jnp.uint32) | jnp.uint32(0x80000000)
    return lax.bitcast_convert_type(bits, F32)


def _rms(x):
    return x * lax.rsqrt(jnp.mean(x * x, axis=-1, keepdims=True) + EPS)


def _split_bf16(v):
    hi = v.astype(BF16)
    lo = (v - hi.astype(F32)).astype(BF16)
    return hi, lo


def _mod_kernel(c_ref, w_ref, b_ref, o_ref):
    c = c_ref[...]
    s = (c * _sigmoid(c)).astype(BF16)
    o_ref[0] = jnp.dot(s, w_ref[0].astype(BF16), preferred_element_type=F32) + b_ref[0]


def _mod_call(cvec, w_ada, b_ada, tn=1024):
    depth, d, n = w_ada.shape
    return pl.pallas_call(
        _mod_kernel,
        out_shape=jax.ShapeDtypeStruct((depth, MOD_ROWS, n), F32),
        grid=(depth, n // tn),
        in_specs=[
            pl.BlockSpec((MOD_ROWS, d), lambda l, i: (0, 0)),
            pl.BlockSpec((1, d, tn), lambda l, i: (l, 0, i)),
            pl.BlockSpec((1, 1, tn), lambda l, i: (l, 0, i)),
        ],
        out_specs=pl.BlockSpec((1, MOD_ROWS, tn), lambda l, i: (l, 0, i)),
        compiler_params=_params("parallel", "parallel"),
        name="adaln_mod",
    )(cvec, w_ada, b_ada.reshape(depth, 1, n))


def _lane_blocks(d):
    return [slice(s, s + LANES) for s in range(0, d, LANES)]


def _row_tiles(n_rows):
    sub = min(EDGE_SUB_ROWS, n_rows)
    return [slice(r, r + sub) for r in range(0, n_rows, sub)]


def _row_scale(src_ref, rows):
    d = src_ref.shape[2]
    acc = None
    for cols in _lane_blocks(d):
        xb = src_ref[0, rows, cols]
        acc = xb * xb if acc is None else acc + xb * xb
    ss = jnp.sum(acc, axis=-1, keepdims=True)
    return jnp.broadcast_to(lax.rsqrt(ss * (1.0 / d) + EPS), acc.shape)


def _modulated_norm(x_ref, rows, gain, shift, scale, h_ref):
    rs = _row_scale(x_ref, rows)
    for cols in _lane_blocks(x_ref.shape[2]):
        mul = gain[:, cols] * (1.0 + scale[:, cols])
        h_ref[rows, cols] = (x_ref[0, rows, cols] * rs * mul + shift[:, cols]).astype(BF16)


def _ffn_kernel(x_ref, mod_ref, g_ref, wg_ref, wu_ref, wo_ref, o_ref, h_ref, *, j):
    f = pl.program_id(2)
    last = pl.num_programs(2) - 1
    tm, d = h_ref.shape

    def swiglu(rows):
        h = h_ref[rows, :]
        gate = jnp.dot(h, wg_ref[...], preferred_element_type=F32)
        up = jnp.dot(h, wu_ref[...], preferred_element_type=F32)
        a = (gate * _sigmoid(gate) * up).astype(BF16)
        return jnp.dot(a, wo_ref[...], preferred_element_type=F32)

    @pl.when(f == 0)
    def _():
        gain = g_ref[pl.ds(2 * j, 1), :]
        shift = mod_ref[0, pl.ds(3 * j, 1), :]
        scale = mod_ref[0, pl.ds(3 * j + 1, 1), :]
        for rows in _row_tiles(tm):
            _modulated_norm(x_ref, rows, gain, shift, scale, h_ref)
            o_ref[0, rows, :] = swiglu(rows)

    @pl.when((f > 0) & (f < last))
    def _():
        o_ref[0] += swiglu(slice(None))

    @pl.when(f == last)
    def _():
        gain = g_ref[pl.ds(2 * j + 1, 1), :]
        half_gate = 0.5 * mod_ref[0, pl.ds(3 * j + 2, 1), :]
        for rows in _row_tiles(tm):
            o_ref[0, rows, :] += swiglu(rows)
            rs = _row_scale(o_ref, rows)
            for cols in _lane_blocks(d):
                mul = half_gate[:, cols] * gain[:, cols]
                o_ref[0, rows, cols] = x_ref[0, rows, cols] + o_ref[0, rows, cols] * rs * mul


def _ffn_call(x, mod, gains, w_in, w_out, layer, which, tf=512):
    b, l, d = x.shape
    d_ff = w_out.shape[2]
    tm = min(1024, l)
    nf = d_ff // tf
    assert nf >= 2, "the kernel's first and last grid steps are distinct code paths"
    return pl.pallas_call(
        functools.partial(_ffn_kernel, j=2 * which),
        out_shape=jax.ShapeDtypeStruct((b, l, d), F32),
        grid=(b, l // tm, nf),
        in_specs=[
            pl.BlockSpec((1, tm, d), lambda bi, i, f: (bi, i, 0)),
            pl.BlockSpec((1, N_MOD, d), lambda bi, i, f: (bi, 0, 0)),
            pl.BlockSpec(gains.shape, lambda bi, i, f: (0, 0)),
            pl.BlockSpec((None, None, d, tf), lambda bi, i, f: (layer, which, 0, f)),
            pl.BlockSpec((None, None, d, tf), lambda bi, i, f: (layer, which, 0, f + nf)),
            pl.BlockSpec((None, None, tf, d), lambda bi, i, f: (layer, which, f, 0)),
        ],
        out_specs=pl.BlockSpec((1, tm, d), lambda bi, i, f: (bi, i, 0)),
        scratch_shapes=[pltpu.VMEM((tm, d), BF16)],
        compiler_params=_params("parallel", "parallel", "arbitrary"),
        name="ffn_sublayer",
    )(x, mod, gains, w_in, w_in, w_out)


def _decay_gates(z, lb, one_m_lb):
    z2 = z * LOG2_E
    e = jnp.exp2(_neg_abs(z2))
    ope = 1.0 + e
    pos = z > 0
    num = jnp.where(pos, 1.0, lb) + jnp.where(pos, lb * e, e)
    log2_num = jnp.where((lb == 0.0) & jnp.logical_not(pos), z2, jnp.log2(num))
    k = jnp.where(pos, e, 1.0) * one_m_lb * (1.0 / ope)
    return k, log2_num - jnp.log2(ope)


def _inproj_kernel(x_ref, mod_ref, g_ref, w_ref, lb_ref, omlb_ref, z_ref, zk_ref, h_ref):
    part = pl.program_id(2)
    row_tiles = _row_tiles(h_ref.shape[0])

    def project(rows):
        return jnp.dot(h_ref[rows, :], w_ref[...], preferred_element_type=F32)

    @pl.when(part == PART_Q)
    def _():
        gain, shift, scale = g_ref[pl.ds(2, 1), :], mod_ref[0, pl.ds(3, 1), :], mod_ref[0, pl.ds(4, 1), :]
        for rows in row_tiles:
            _modulated_norm(x_ref, rows, gain, shift, scale, h_ref)
            a = project(rows)
            z_ref[0, rows, :] = a * _sigmoid(a)

    @pl.when(part == PART_GATE)
    def _():
        for rows in row_tiles:
            a = project(rows)
            z_ref[0, rows, :] = a * _sigmoid(a)

    for direction, forget_part in enumerate((PART_F_FWD, PART_F_BWD)):
        @pl.when(part == forget_part)
        def _(direction=direction):
            for rows in row_tiles:
                k, log2_f = _decay_gates(project(rows), lb_ref[direction:direction + 1, :],
                                         omlb_ref[direction:direction + 1, :])
                z_ref[0, rows, :] = log2_f
                zk_ref[0, rows, :] = k

    @pl.when((part == PART_V) | (part == PART_POOL))
    def _():
        z_ref[0] = project(slice(None))


def _inproj_call(x, mod, gains, w, layer, lb, one_m_lb):
    b, l, d = x.shape
    n = w.shape[2]
    width = lb.shape[1]
    tm = min(1024, l)
    lb_spec = pl.BlockSpec(lb.shape, lambda bi, i, k: (0, 0))
    return pl.pallas_call(
        _inproj_kernel,
        out_shape=(jax.ShapeDtypeStruct((b, l, n), F32),
                   jax.ShapeDtypeStruct((b, l, 2 * width), F32)),
        grid=(b, l // tm, n // width),
        in_specs=[
            pl.BlockSpec((1, tm, d), lambda bi, i, k: (bi, i, 0)),
            pl.BlockSpec((1, N_MOD, d), lambda bi, i, k: (bi, 0, 0)),
            pl.BlockSpec(gains.shape, lambda bi, i, k: (0, 0)),
            pl.BlockSpec((None, d, width), lambda bi, i, k: (layer, 0, k)),
            lb_spec, lb_spec,
        ],
        out_specs=(pl.BlockSpec((1, tm, width), lambda bi, i, k: (bi, i, k)),
                   pl.BlockSpec((1, tm, width),
                                lambda bi, i, k: (bi, i, jnp.clip(k - PART_F_FWD, 0, 1)))),
        scratch_shapes=[pltpu.VMEM((tm, d), BF16)],
        compiler_params=_params("parallel", "parallel", "arbitrary"),
        name="mixer_in_proj",
    )(x, mod, gains, w, lb, one_m_lb)


def _mid_rows(b_ref, slot, c, half):
    if half >= 4:
        blk = 2 * half
        return jnp.concatenate(
            [jnp.broadcast_to(b_ref[slot, pl.ds(s + half - 1, 1), :], (blk, HEAD_DIM))
             for s in range(0, c, blk)], axis=0)
    sub = lax.broadcasted_iota(jnp.int32, (8, HEAD_DIM), 0)
    vregs = []
    for s in range(0, c, 8):
        lo = jnp.broadcast_to(b_ref[slot, pl.ds(s + 1, 1), :], (8, HEAD_DIM))
        hi = jnp.broadcast_to(b_ref[slot, pl.ds(s + 5, 1), :], (8, HEAD_DIM))
        vregs.append(jnp.where(sub < 4, lo, hi))
    return jnp.concatenate(vregs, axis=0)


def _pick_halves(even_src, odd_src, half, odd_mask):
    c = even_src.shape[0]
    if half < 8:
        return jnp.where(odd_mask, odd_src, even_src)
    shp = (c // (2 * half), 2, half, HEAD_DIM)
    both = jnp.stack([even_src.reshape(shp)[:, 0], odd_src.reshape(shp)[:, 1]], axis=1)
    return both.reshape(c, HEAD_DIM)


def _hgrn_direction(q, k, v, g, b_ref, slot, code_ref, st, reverse):
    c = q.shape[0]
    hc = c // 2
    b = b_ref[slot]
    beta = b - g if reverse else b
    row = lax.broadcasted_iota(jnp.int32, (c, HEAD_DIM), 0)
    first = jnp.zeros((hc, hc), F32)
    second = jnp.zeros((hc, hc), F32)
    half = 1
    while half < hc:
        odd = (row & half) != 0 if half < 8 else None
        if half == 1:
            u = jnp.where(odd, 0.0, g) if reverse else jnp.where(odd, g, 0.0)
        else:
            u = _neg_abs(beta - _mid_rows(b_ref, slot, c, half))
        src = _pick_halves(q, k, half, odd) if reverse else _pick_halves(k, q, half, odd)
        r = (src * jnp.exp2(u)).astype(BF16)
        a = lax.dot_general(r, r, (((1,), (1,)), ((), ())), preferred_element_type=F32)
        first = jnp.where(code_ref[int(reverse)] >= half, a[:hc, :hc], first)
        second = jnp.where(code_ref[int(reverse)] >= half, a[hc:, hc:], second)
        half *= 2
    u = _neg_abs(beta - _mid_rows(b_ref, slot, c, hc))
    src = _pick_halves(q, k, hc, None) if reverse else _pick_halves(k, q, hc, None)
    r = (src * jnp.exp2(u)).astype(BF16)
    a = lax.dot_general(r, r, (((1,), (1,)), ((), ())), preferred_element_type=F32)

    tot = b_ref[slot, pl.ds(c - 1, 1), :]
    q_in, k_out = (tot - beta, beta) if reverse else (b, tot - b)
    q_state = (q * jnp.exp2(q_in)).astype(BF16)
    k_state = (k * jnp.exp2(k_out)).astype(BF16)
    vb = v.astype(BF16)
    if reverse:
        top = jnp.concatenate([first, a[:hc, hc:]], axis=1).astype(BF16)
        o = jnp.concatenate([jnp.dot(top, vb, preferred_element_type=F32),
                             jnp.dot(second.astype(BF16), vb[hc:], preferred_element_type=F32)], axis=0)
    else:
        bottom = jnp.concatenate([a[hc:, :hc], second], axis=1).astype(BF16)
        o = jnp.concatenate([jnp.dot(first.astype(BF16), vb[:hc], preferred_element_type=F32),
                             jnp.dot(bottom, vb, preferred_element_type=F32)], axis=0)
    o += lax.dot_general(q_state, st.astype(BF16), (((1,), (1,)), ((), ())),
                         preferred_element_type=F32)
    o += jnp.sum(q * k, axis=-1, keepdims=True) * v
    st_new = st * jnp.exp2(tot) + lax.dot_general(
        vb, k_state, (((0,), (0,)), ((), ())), preferred_element_type=F32)
    return o, st_new


def _hgrn_kernel(qf_ref, gf_ref, kf_ref, vf_ref, qb_ref, gb_ref, kb_ref, vb_ref,
                 s0f_ref, s0b_ref, of_ref, ob_ref, sf_ref, sb_ref, st_ref, b_ref, code_ref, tri_ref):
    j = pl.program_id(2)
    c = qf_ref.shape[1]
    hp = qf_ref.shape[2] // HEAD_DIM
    lanes = [slice(h * HEAD_DIM, (h + 1) * HEAD_DIM) for h in range(hp)]

    @pl.when(j == 0)
    def _():
        st_ref[0:hp] = s0f_ref[0]
        st_ref[hp:2 * hp] = s0b_ref[0]
        ti = lax.broadcasted_iota(jnp.int32, (c, c), 0)
        si = lax.broadcasted_iota(jnp.int32, (c, c), 1)
        tri_ref[...] = jnp.where(ti >= si, 1.0, 0.0).astype(BF16)
        ti = lax.broadcasted_iota(jnp.int32, (c // 2, c // 2), 0)
        si = lax.broadcasted_iota(jnp.int32, (c // 2, c // 2), 1)
        level = ti ^ si
        code_ref[0] = jnp.where(ti > si, level, 0)
        code_ref[1] = jnp.where(ti < si, level, 0)

    hi, lo = _split_bf16(jnp.concatenate([gf_ref[0], gb_ref[0]], axis=1))
    b2 = (jnp.dot(tri_ref[...], hi, preferred_element_type=F32)
          + jnp.dot(tri_ref[...], lo, preferred_element_type=F32))
    for slot in range(2 * hp):
        b_ref[slot] = b2[:, slot * HEAD_DIM:(slot + 1) * HEAD_DIM]

    for h, cols in enumerate(lanes):
        o, st = _hgrn_direction(qf_ref[0, :, cols], kf_ref[0, :, cols], vf_ref[0, :, cols],
                                gf_ref[0, :, cols], b_ref, h, code_ref, st_ref[h], False)
        of_ref[0, :, cols] = o
        st_ref[h] = st
        o, st = _hgrn_direction(qb_ref[0, :, cols], kb_ref[0, :, cols], vb_ref[0, :, cols],
                                gb_ref[0, :, cols], b_ref, hp + h, code_ref, st_ref[hp + h], True)
        ob_ref[0, :, cols] = o
        st_ref[hp + h] = st

    @pl.when(j == pl.num_programs(2) - 1)
    def _():
        sf_ref[0] = st_ref[0:hp]
        sb_ref[0] = st_ref[hp:2 * hp]


def _hgrn_call(z, zk, s0f, s0b):
    b, l, _ = z.shape
    width = zk.shape[2] // 2
    heads = width // HEAD_DIM
    hp = min(HGRN_HEADS_PER_STEP, heads)
    groups = heads // hp
    c = min(HGRN_CHUNK, l)
    n = l // c
    col = lambda part: (lambda bi, h, j: (bi, j, part * groups + h))
    col_rev = lambda part: (lambda bi, h, j: (bi, n - 1 - j, part * groups + h))
    tok = lambda imap: pl.BlockSpec((1, c, hp * HEAD_DIM), imap)
    st_spec = pl.BlockSpec((1, hp, HEAD_DIM, HEAD_DIM), lambda bi, h, j: (bi, h, 0, 0))
    st_shape = jax.ShapeDtypeStruct((b, heads, HEAD_DIM, HEAD_DIM), F32)
    return pl.pallas_call(
        _hgrn_kernel,
        out_shape=(jax.ShapeDtypeStruct((b, l, width), F32),
                   jax.ShapeDtypeStruct((b, l, width), F32), st_shape, st_shape),
        grid=(b, groups, n),
        in_specs=[tok(col(PART_Q)), tok(col(PART_F_FWD)), tok(col(0)), tok(col(PART_V)),
                  tok(col_rev(PART_Q)), tok(col_rev(PART_F_BWD)), tok(col_rev(1)), tok(col_rev(PART_V)),
                  st_spec, st_spec],
        out_specs=(tok(lambda bi, h, j: (bi, j, h)), tok(lambda bi, h, j: (bi, n - 1 - j, h)),
                   st_spec, st_spec),
        scratch_shapes=[pltpu.VMEM((2 * hp, HEAD_DIM, HEAD_DIM), F32),
                        pltpu.VMEM((2 * hp, c, HEAD_DIM), F32),
                        pltpu.VMEM((2, c // 2, c // 2), jnp.int32),
                        pltpu.VMEM((c, c), BF16)],
        compiler_params=_params("parallel", "parallel", "arbitrary"),
        name="hgrn2_scan",
    )(z, z, zk, z, z, z, zk, z, s0f, s0b)


def _pool_project(m, v, g, wp_ref, bp_ref, ps_ref, y_ref):
    cols = slice(g * POOL_GROUP_DIM, (g + 1) * POOL_GROUP_DIM)
    y = jnp.dot((m - v).astype(BF16), wp_ref[g], preferred_element_type=F32)
    y_ref[0, :, cols] = ((y + bp_ref[:, cols]) * ps_ref[:, cols]).astype(y_ref.dtype)


def _pool2d_kernel(prev_ref, cur_ref, next_ref, wp_ref, bp_ref, ps_ref, y_ref, cs_ref, *, rows):
    i = pl.program_id(1)
    tile = cur_ref.shape[1]
    halo = prev_ref.shape[1]
    ext = halo + tile + halo
    first_row = i * POOL_TILE_ROWS - POOL_HALO_ROWS

    tok = lax.broadcasted_iota(jnp.int32, (tile, POOL_GROUP_DIM), 0)
    grow = tok // GRID_W + i * POOL_TILE_ROWS
    gcol = tok % GRID_W
    eo = lax.broadcasted_iota(jnp.int32, (COL_BLOCK, COL_BLOCK), 0)
    ei = lax.broadcasted_iota(jnp.int32, (COL_BLOCK, COL_BLOCK), 1)
    same_row = (eo // GRID_W) == (ei // GRID_W)
    dcol = ei % GRID_W - eo % GRID_W
    erow = lax.broadcasted_iota(jnp.int32, (halo, POOL_GROUP_DIM), 0) // GRID_W

    for g, w in enumerate(POOL_WINDOWS):
        cols = slice(g * POOL_GROUP_DIM, (g + 1) * POOL_GROUP_DIM)
        back = w // 2
        band = jnp.where(same_row & (dcol >= -back) & (dcol < w - back), 1.0, 0.0).astype(BF16)
        pieces = (
            jnp.where(erow + first_row >= 0, prev_ref[0, :, cols], 0.0),
            cur_ref[0, :, cols],
            jnp.where(erow + first_row + POOL_HALO_ROWS + POOL_TILE_ROWS < rows,
                      next_ref[0, :, cols], 0.0),
        )
        off = 0
        for piece in pieces:
            for s in range(0, piece.shape[0], COL_BLOCK):
                hi, lo = _split_bf16(piece[s:s + COL_BLOCK])
                cs_ref[off + s:off + s + COL_BLOCK, :] = (
                    jnp.dot(band, hi, preferred_element_type=F32)
                    + jnp.dot(band, lo, preferred_element_type=F32))
            off += piece.shape[0]
        acc = cs_ref[...]
        span = 1
        while span < w:
            n_tok = ext - (2 * span - 1) * GRID_W
            acc = acc[:n_tok] + acc[span * GRID_W:span * GRID_W + n_tok]
            span *= 2
        start = (POOL_HALO_ROWS - back) * GRID_W
        total = acc[start:start + tile]
        cnt_r = jnp.minimum(grow + (w - back), rows) - jnp.maximum(grow - back, 0)
        cnt_c = jnp.minimum(gcol + (w - back), GRID_W) - jnp.maximum(gcol - back, 0)
        m = total / (cnt_r * cnt_c).astype(F32)
        _pool_project(m, cur_ref[0, :, cols], g, wp_ref, bp_ref, ps_ref, y_ref)


def _pool2d_call(z, w_pool, b_pool, pool_scale, rows):
    b, l, n = z.shape
    width = w_pool.shape[0] * POOL_GROUP_DIM
    p_blk = PART_POOL
    tile = POOL_TILE_ROWS * GRID_W
    halo = POOL_HALO_ROWS * GRID_W
    n_halo = l // halo
    per = tile // halo
    vec = pl.BlockSpec((1, width), lambda bi, i: (0, 0))
    return pl.pallas_call(
        functools.partial(_pool2d_kernel, rows=rows),
        out_shape=jax.ShapeDtypeStruct((b, l, width), BF16),
        grid=(b, l // tile),
        in_specs=[
            pl.BlockSpec((1, halo, width), lambda bi, i: (bi, jnp.maximum(i * per - 1, 0), p_blk)),
            pl.BlockSpec((1, tile, width), lambda bi, i: (bi, i, p_blk)),
            pl.BlockSpec((1, halo, width),
                         lambda bi, i: (bi, jnp.minimum((i + 1) * per, n_halo - 1), p_blk)),
            pl.BlockSpec(w_pool.shape, lambda bi, i: (0, 0, 0)),
            vec, vec,
        ],
        out_specs=pl.BlockSpec((1, tile, width), lambda bi, i: (bi, i, 0)),
        scratch_shapes=[pltpu.VMEM((halo + tile + halo, POOL_GROUP_DIM), F32)],
        compiler_params=_params("parallel", "parallel"),
        name="pool_mixer_2d",
    )(z, z, z, w_pool, b_pool, pool_scale)


def _pool1d_kernel(p_ref, wp_ref, bp_ref, ps_ref, y_ref):
    l = p_ref.shape[1]
    to = lax.broadcasted_iota(jnp.int32, (l, l), 0)
    dt = lax.broadcasted_iota(jnp.int32, (l, l), 1) - to
    pos = lax.broadcasted_iota(jnp.int32, (l, POOL_GROUP_DIM), 0)
    for g, w in enumerate(POOL_WINDOWS):
        cols = slice(g * POOL_GROUP_DIM, (g + 1) * POOL_GROUP_DIM)
        back = w // 2
        band = jnp.where((dt >= -back) & (dt < w - back), 1.0, 0.0).astype(BF16)
        v = p_ref[0, :, cols]
        hi, lo = _split_bf16(v)
        total = (jnp.dot(band, hi, preferred_element_type=F32)
                 + jnp.dot(band, lo, preferred_element_type=F32))
        cnt = jnp.minimum(pos + (w - back), l) - jnp.maximum(pos - back, 0)
        _pool_project(total / cnt.astype(F32), v, g, wp_ref, bp_ref, ps_ref, y_ref)


def _pool1d_call(z, w_pool, b_pool, pool_scale):
    b, l, n = z.shape
    width = w_pool.shape[0] * POOL_GROUP_DIM
    p_blk = PART_POOL
    vec = pl.BlockSpec((1, width), lambda bi: (0, 0))
    return pl.pallas_call(
        _pool1d_kernel,
        out_shape=jax.ShapeDtypeStruct((b, l, width), BF16),
        grid=(b,),
        in_specs=[pl.BlockSpec((1, l, width), lambda bi: (bi, 0, p_blk)),
                  pl.BlockSpec(w_pool.shape, lambda bi: (0, 0, 0)), vec, vec],
        out_specs=pl.BlockSpec((1, l, width), lambda bi: (bi, 0, 0)),
        compiler_params=_params("parallel"),
        name="pool_mixer_1d",
    )(z, w_pool, b_pool, pool_scale)


def _outproj_kernel(of_ref, ob_ref, gate_ref, y_ref, x_ref, mod_ref, g_ref, hg_ref, w_ref, o_ref):
    width = of_ref.shape[2]
    o = of_ref[0] + ob_ref[0]
    heads = []
    for h in range(width // HEAD_DIM):
        heads.append(_rms(o[:, h * HEAD_DIM:(h + 1) * HEAD_DIM]) * hg_ref[...])
    og = (jnp.concatenate(heads, axis=-1) * gate_ref[0]).astype(BF16)
    mix = jnp.dot(jnp.concatenate([og, y_ref[0]], axis=-1), w_ref[...], preferred_element_type=F32)
    r = _rms(mix) * g_ref[pl.ds(3, 1), :]
    o_ref[0] = x_ref[0] + mod_ref[0, pl.ds(5, 1), :] * r


def _outproj_call(o_fw, o_bw, z, y, x, mod, gains, hgrn_gain, w, layer):
    b, l, d = x.shape
    width = o_fw.shape[2]
    g_blk = PART_GATE
    tm = min(512, l)
    tokw = lambda imap: pl.BlockSpec((1, tm, width), imap)
    row = lambda bi, i: (bi, i, 0)
    return pl.pallas_call(
        _outproj_kernel,
        out_shape=jax.ShapeDtypeStruct((b, l, d), F32),
        grid=(b, l // tm),
        in_specs=[
            tokw(row), tokw(row), tokw(lambda bi, i: (bi, i, g_blk)), tokw(row),
            pl.BlockSpec((1, tm, d), row),
            pl.BlockSpec((1, N_MOD, d), lambda bi, i: (bi, 0, 0)),
            pl.BlockSpec(gains.shape, lambda bi, i: (0, 0)),
            pl.BlockSpec((1, HEAD_DIM), lambda bi, i: (0, 0)),
            pl.BlockSpec((None,) + w.shape[1:], lambda bi, i: (layer, 0, 0)),
        ],
        out_specs=pl.BlockSpec((1, tm, d), row),
        compiler_params=_params("parallel", "parallel"),
        name="mixer_out_proj",
    )(o_fw, o_bw, z, y, x, mod, gains, hgrn_gain, w)


def kernel(x, c, ctx, c_ctx, w_ada, b_ada, norm_gain, ffn_in, ffn_out, w_in, hgrn_lb, hgrn_gain,
           w_pool, b_pool, pool_scale, w_out):
    bsz, seq, d = x.shape
    depth = w_ada.shape[0]
    rows = seq // GRID_W
    width = hgrn_lb.shape[-1]
    heads = width // HEAD_DIM

    lbp = jax.nn.softmax(hgrn_lb.astype(F32), axis=0)
    lower = jnp.cumsum(lbp, axis=0) - lbp[0]
    one_m_lb = 1.0 - lower

    cvec = jnp.zeros((MOD_ROWS, d), F32).at[:bsz].set(c).at[bsz].set(c_ctx)
    mod_all = _mod_call(cvec, w_ada, b_ada)

    ffn_in_b, ffn_out_b = ffn_in.astype(BF16), ffn_out.astype(BF16)
    w_in_b, w_out_b, w_pool_b = w_in.astype(BF16), w_out.astype(BF16), w_pool.astype(BF16)

    s0 = jnp.zeros((bsz, heads, HEAD_DIM, HEAD_DIM), F32)
    for l in range(depth):
        last = l == depth - 1
        gains = norm_gain[l]
        mod_x = mod_all[l, :bsz].reshape(bsz, N_MOD, d)
        mod_c = jnp.broadcast_to(mod_all[l, bsz].reshape(1, N_MOD, d), (bsz, N_MOD, d))
        hg = hgrn_gain[l].reshape(1, HEAD_DIM)
        bp = b_pool[l].reshape(1, -1)
        ps = pool_scale[l].reshape(1, -1)

        x = _ffn_call(x, mod_x, gains, ffn_in_b, ffn_out_b, l, 0)
        ctx = _ffn_call(ctx, mod_c, gains, ffn_in_b, ffn_out_b, l, 0)

        z_x, zk_x = _inproj_call(x, mod_x, gains, w_in_b, l, lower[l], one_m_lb[l])
        z_c, zk_c = _inproj_call(ctx, mod_c, gains, w_in_b, l, lower[l], one_m_lb[l])
        oc_fw, oc_bw, s_fw, s_bw = _hgrn_call(z_c, zk_c, s0, s0)
        ox_fw, ox_bw, _, _ = _hgrn_call(z_x, zk_x, s_fw, s_bw)
        y_x = _pool2d_call(z_x, w_pool_b[l], bp, ps, rows)
        x = _outproj_call(ox_fw, ox_bw, z_x, y_x, x, mod_x, gains, hg, w_out_b, l)
        if not last:
            y_c = _pool1d_call(z_c, w_pool_b[l], bp, ps)
            ctx = _outproj_call(oc_fw, oc_bw, z_c, y_c, ctx, mod_c, gains, hg, w_out_b, l)
            ctx = _ffn_call(ctx, mod_c, gains, ffn_in_b, ffn_out_b, l, 1)

        x = _ffn_call(x, mod_x, gains, ffn_in_b, ffn_out_b, l, 1)
    return x
```

```python
import functools

import jax
import jax.numpy as jnp
from jax import lax
from jax.experimental import pallas as pl
from jax.experimental.pallas import tpu as pltpu

F32 = jnp.float32
BF16 = jnp.bfloat16

HEAD_DIM = 128
GRID_W = 64
POOL_WINDOWS = (2, 4, 8, 16)
POOL_GROUP_DIM = 256
N_MOD = 9
EPS = 1e-6
LOG2_E = 1.4426950408889634
PART_Q, PART_F_FWD, PART_F_BWD, PART_V, PART_GATE, PART_POOL = range(6)
V7X_VMEM_LIMIT_BYTES = 56 * 1024 * 1024
MOD_ROWS = 8
HGRN_CHUNK = 256
HGRN_HEADS_PER_STEP = 8
POOL_TILE_ROWS = 16
POOL_HALO_ROWS = 8
COL_BLOCK = 256
LANES = 128
EDGE_SUB_ROWS = 256

def _params(*semantics):
    return pltpu.CompilerParams(dimension_semantics=semantics,
                                vmem_limit_bytes=V7X_VMEM_LIMIT_BYTES)


def _sigmoid(x):
    return 1.0 / (1.0 + jnp.exp(-x))


def _neg_abs(d):
    bits = lax.bitcast_convert_type(d, jnp.uint32) | jnp.uint32(0x80000000)
    return lax.bitcast_convert_type(bits, F32)


def _rms(x):
    return x * lax.rsqrt(jnp.mean(x * x, axis=-1, keepdims=True) + EPS)


def _split_bf16(v):
    hi = v.astype(BF16)
    lo = (v - hi.astype(F32)).astype(BF16)
    return hi, lo


def _mod_kernel(c_ref, w_ref, b_ref, o_ref):
    c = c_ref[...]
    s = (c * _sigmoid(c)).astype(BF16)
    o_ref[0] = jnp.dot(s, w_ref[0].astype(BF16), preferred_element_type=F32) + b_ref[0]


def _mod_call(cvec, w_ada, b_ada, tn=1024):
    depth, d, n = w_ada.shape
    return pl.pallas_call(
        _mod_kernel,
        out_shape=jax.ShapeDtypeStruct((depth, MOD_ROWS, n), F32),
        grid=(depth, n // tn),
        in_specs=[
            pl.BlockSpec((MOD_ROWS, d), lambda l, i: (0, 0)),
            pl.BlockSpec((1, d, tn), lambda l, i: (l, 0, i)),
            pl.BlockSpec((1, 1, tn), lambda l, i: (l, 0, i)),
        ],
        out_specs=pl.BlockSpec((1, MOD_ROWS, tn), lambda l, i: (l, 0, i)),
        compiler_params=_params("parallel", "parallel"),
        name="adaln_mod",
    )(cvec, w_ada, b_ada.reshape(depth, 1, n))


def _lane_blocks(d):
    return [slice(s, s + LANES) for s in range(0, d, LANES)]


def _row_tiles(n_rows):
    sub = min(EDGE_SUB_ROWS, n_rows)
    return [slice(r, r + sub) for r in range(0, n_rows, sub)]


def _row_scale(src_ref, rows):
    d = src_ref.shape[2]
    acc = None
    for cols in _lane_blocks(d):
        xb = src_ref[0, rows, cols]
        acc = xb * xb if acc is None else acc + xb * xb
    ss = jnp.sum(acc, axis=-1, keepdims=True)
    return jnp.broadcast_to(lax.rsqrt(ss * (1.0 / d) + EPS), acc.shape)


def _modulated_norm(x_ref, rows, gain, shift, scale, h_ref):
    rs = _row_scale(x_ref, rows)
    for cols in _lane_blocks(x_ref.shape[2]):
        mul = gain[:, cols] * (1.0 + scale[:, cols])
        h_ref[rows, cols] = (x_ref[0, rows, cols] * rs * mul + shift[:, cols]).astype(BF16)


def _ffn_kernel(x_ref, mod_ref, g_ref, wg_ref, wu_ref, wo_ref, o_ref, h_ref, *, j):
    f = pl.program_id(2)
    last = pl.num_programs(2) - 1
    tm, d = h_ref.shape

    def swiglu(rows):
        h = h_ref[rows, :]
        gate = jnp.dot(h, wg_ref[...], preferred_element_type=F32)
        up = jnp.dot(h, wu_ref[...], preferred_element_type=F32)
        a = (gate * _sigmoid(gate) * up).astype(BF16)
        return jnp.dot(a, wo_ref[...], preferred_element_type=F32)

    @pl.when(f == 0)
    def _():
        gain = g_ref[pl.ds(2 * j, 1), :]
        shift = mod_ref[0, pl.ds(3 * j, 1), :]
        scale = mod_ref[0, pl.ds(3 * j + 1, 1), :]
        for rows in _row_tiles(tm):
            _modulated_norm(x_ref, rows, gain, shift, scale, h_ref)
            o_ref[0, rows, :] = swiglu(rows)

    @pl.when((f > 0) & (f < last))
    def _():
        o_ref[0] += swiglu(slice(None))

    @pl.when(f == last)
    def _():
        gain = g_ref[pl.ds(2 * j + 1, 1), :]
        half_gate = 0.5 * mod_ref[0, pl.ds(3 * j + 2, 1), :]
        for rows in _row_tiles(tm):
            o_ref[0, rows, :] += swiglu(rows)
            rs = _row_scale(o_ref, rows)
            for cols in _lane_blocks(d):
                mul = half_gate[:, cols] * gain[:, cols]
                o_ref[0, rows, cols] = x_ref[0, rows, cols] + o_ref[0, rows, cols] * rs * mul


def _ffn_call(x, mod, gains, w_in, w_out, layer, which, tf=512):
    b, l, d = x.shape
    d_ff = w_out.shape[2]
    tm = min(1024, l)
    nf = d_ff // tf
    assert nf >= 2, "the kernel's first and last grid steps are distinct code paths"
    return pl.pallas_call(
        functools.partial(_ffn_kernel, j=2 * which),
        out_shape=jax.ShapeDtypeStruct((b, l, d), F32),
        grid=(b, l // tm, nf),
        in_specs=[
            pl.BlockSpec((1, tm, d), lambda bi, i, f: (bi, i, 0)),
            pl.BlockSpec((1, N_MOD, d), lambda bi, i, f: (bi, 0, 0)),
            pl.BlockSpec(gains.shape, lambda bi, i, f: (0, 0)),
            pl.BlockSpec((None, None, d, tf), lambda bi, i, f: (layer, which, 0, f)),
            pl.BlockSpec((None, None, d, tf), lambda bi, i, f: (layer, which, 0, f + nf)),
            pl.BlockSpec((None, None, tf, d), lambda bi, i, f: (layer, which, f, 0)),
        ],
        out_specs=pl.BlockSpec((1, tm, d), lambda bi, i, f: (bi, i, 0)),
        scratch_shapes=[pltpu.VMEM((tm, d), BF16)],
        compiler_params=_params("parallel", "parallel", "arbitrary"),
        name="ffn_sublayer",
    )(x, mod, gains, w_in, w_in, w_out)


def _decay_gates(z, lb, one_m_lb):
    z2 = z * LOG2_E
    e = jnp.exp2(_neg_abs(z2))
    ope = 1.0 + e
    pos = z > 0
    num = jnp.where(pos, 1.0, lb) + jnp.where(pos, lb * e, e)
    log2_num = jnp.where((lb == 0.0) & jnp.logical_not(pos), z2, jnp.log2(num))
    k = jnp.where(pos, e, 1.0) * one_m_lb * (1.0 / ope)
    return k, log2_num - jnp.log2(ope)


def _inproj_kernel(x_ref, mod_ref, g_ref, w_ref, lb_ref, omlb_ref, z_ref, zk_ref, h_ref):
    part = pl.program_id(2)
    row_tiles = _row_tiles(h_ref.shape[0])

    def project(rows):
        return jnp.dot(h_ref[rows, :], w_ref[...], preferred_element_type=F32)

    @pl.when(part == PART_Q)
    def _():
        gain, shift, scale = g_ref[pl.ds(2, 1), :], mod_ref[0, pl.ds(3, 1), :], mod_ref[0, pl.ds(4, 1), :]
        for rows in row_tiles:
            _modulated_norm(x_ref, rows, gain, shift, scale, h_ref)
            a = project(rows)
            z_ref[0, rows, :] = a * _sigmoid(a)

    @pl.when(part == PART_GATE)
    def _():
        for rows in row_tiles:
            a = project(rows)
            z_ref[0, rows, :] = a * _sigmoid(a)

    for direction, forget_part in enumerate((PART_F_FWD, PART_F_BWD)):
        @pl.when(part == forget_part)
        def _(direction=direction):
            for rows in row_tiles:
                k, log2_f = _decay_gates(project(rows), lb_ref[direction:direction + 1, :],
                                         omlb_ref[direction:direction + 1, :])
                z_ref[0, rows, :] = log2_f
                zk_ref[0, rows, :] = k

    @pl.when((part == PART_V) | (part == PART_POOL))
    def _():
        z_ref[0] = project(slice(None))


def _inproj_call(x, mod, gains, w, layer, lb, one_m_lb):
    b, l, d = x.shape
    n = w.shape[2]
    width = lb.shape[1]
    tm = min(1024, l)
    lb_spec = pl.BlockSpec(lb.shape, lambda bi, i, k: (0, 0))
    return pl.pallas_call(
        _inproj_kernel,
        out_shape=(jax.ShapeDtypeStruct((b, l, n), F32),
                   jax.ShapeDtypeStruct((b, l, 2 * width), F32)),
        grid=(b, l // tm, n // width),
        in_specs=[
            pl.BlockSpec((1, tm, d), lambda bi, i, k: (bi, i, 0)),
            pl.BlockSpec((1, N_MOD, d), lambda bi, i, k: (bi, 0, 0)),
            pl.BlockSpec(gains.shape, lambda bi, i, k: (0, 0)),
            pl.BlockSpec((None, d, width), lambda bi, i, k: (layer, 0, k)),
            lb_spec, lb_spec,
        ],
        out_specs=(pl.BlockSpec((1, tm, width), lambda bi, i, k: (bi, i, k)),
                   pl.BlockSpec((1, tm, width),
                                lambda bi, i, k: (bi, i, jnp.clip(k - PART_F_FWD, 0, 1)))),
        scratch_shapes=[pltpu.VMEM((tm, d), BF16)],
        compiler_params=_params("parallel", "parallel", "arbitrary"),
        name="mixer_in_proj",
    )(x, mod, gains, w, lb, one_m_lb)


def _mid_rows(b_ref, slot, c, half):
    if half >= 4:
        blk = 2 * half
        return jnp.concatenate(
            [jnp.broadcast_to(b_ref[slot, pl.ds(s + half - 1, 1), :], (blk, HEAD_DIM))
             for s in range(0, c, blk)], axis=0)
    sub = lax.broadcasted_iota(jnp.int32, (8, HEAD_DIM), 0)
    vregs = []
    for s in range(0, c, 8):
        lo = jnp.broadcast_to(b_ref[slot, pl.ds(s + 1, 1), :], (8, HEAD_DIM))
        hi = jnp.broadcast_to(b_ref[slot, pl.ds(s + 5, 1), :], (8, HEAD_DIM))
        vregs.append(jnp.where(sub < 4, lo, hi))
    return jnp.concatenate(vregs, axis=0)


def _pick_halves(even_src, odd_src, half, odd_mask):
    c = even_src.shape[0]
    if half < 8:
        return jnp.where(odd_mask, odd_src, even_src)
    shp = (c // (2 * half), 2, half, HEAD_DIM)
    both = jnp.stack([even_src.reshape(shp)[:, 0], odd_src.reshape(shp)[:, 1]], axis=1)
    return both.reshape(c, HEAD_DIM)


def _hgrn_direction(q, k, v, g, b_ref, slot, code_ref, st, reverse):
    c = q.shape[0]
    hc = c // 2
    b = b_ref[slot]
    beta = b - g if reverse else b
    row = lax.broadcasted_iota(jnp.int32, (c, HEAD_DIM), 0)
    first = jnp.zeros((hc, hc), F32)
    second = jnp.zeros((hc, hc), F32)
    half = 1
    while half < hc:
        odd = (row & half) != 0 if half < 8 else None
        if half == 1:
            u = jnp.where(odd, 0.0, g) if reverse else jnp.where(odd, g, 0.0)
        else:
            u = _neg_abs(beta - _mid_rows(b_ref, slot, c, half))
        src = _pick_halves(q, k, half, odd) if reverse else _pick_halves(k, q, half, odd)
        r = (src * jnp.exp2(u)).astype(BF16)
        a = lax.dot_general(r, r, (((1,), (1,)), ((), ())), preferred_element_type=F32)
        first = jnp.where(code_ref[int(reverse)] >= half, a[:hc, :hc], first)
        second = jnp.where(code_ref[int(reverse)] >= half, a[hc:, hc:], second)
        half *= 2
    u = _neg_abs(beta - _mid_rows(b_ref, slot, c, hc))
    src = _pick_halves(q, k, hc, None) if reverse else _pick_halves(k, q, hc, None)
    r = (src * jnp.exp2(u)).astype(BF16)
    a = lax.dot_general(r, r, (((1,), (1,)), ((), ())), preferred_element_type=F32)

    tot = b_ref[slot, pl.ds(c - 1, 1), :]
    q_in, k_out = (tot - beta, beta) if reverse else (b, tot - b)
    q_state = (q * jnp.exp2(q_in)).astype(BF16)
    k_state = (k * jnp.exp2(k_out)).astype(BF16)
    vb = v.astype(BF16)
    if reverse:
        top = jnp.concatenate([first, a[:hc, hc:]], axis=1).astype(BF16)
        o = jnp.concatenate([jnp.dot(top, vb, preferred_element_type=F32),
                             jnp.dot(second.astype(BF16), vb[hc:], preferred_element_type=F32)], axis=0)
    else:
        bottom = jnp.concatenate([a[hc:, :hc], second], axis=1).astype(BF16)
        o = jnp.concatenate([jnp.dot(first.astype(BF16), vb[:hc], preferred_element_type=F32),
                             jnp.dot(bottom, vb, preferred_element_type=F32)], axis=0)
    o += lax.dot_general(q_state, st.astype(BF16), (((1,), (1,)), ((), ())),
                         preferred_element_type=F32)
    o += jnp.sum(q * k, axis=-1, keepdims=True) * v
    st_new = st * jnp.exp2(tot) + lax.dot_general(
        vb, k_state, (((0,), (0,)), ((), ())), preferred_element_type=F32)
    return o, st_new


def _hgrn_kernel(qf_ref, gf_ref, kf_ref, vf_ref, qb_ref, gb_ref, kb_ref, vb_ref,
                 s0f_ref, s0b_ref, of_ref, ob_ref, sf_ref, sb_ref, st_ref, b_ref, code_ref, tri_ref):
    j = pl.program_id(2)
    c = qf_ref.shape[1]
    hp = qf_ref.shape[2] // HEAD_DIM
    lanes = [slice(h * HEAD_DIM, (h + 1) * HEAD_DIM) for h in range(hp)]

    @pl.when(j == 0)
    def _():
        st_ref[0:hp] = s0f_ref[0]
        st_ref[hp:2 * hp] = s0b_ref[0]
        ti = lax.broadcasted_iota(jnp.int32, (c, c), 0)
        si = lax.broadcasted_iota(jnp.int32, (c, c), 1)
        tri_ref[...] = jnp.where(ti >= si, 1.0, 0.0).astype(BF16)
        ti = lax.broadcasted_iota(jnp.int32, (c // 2, c // 2), 0)
        si = lax.broadcasted_iota(jnp.int32, (c // 2, c // 2), 1)
        level = ti ^ si
        code_ref[0] = jnp.where(ti > si, level, 0)
        code_ref[1] = jnp.where(ti < si, level, 0)

    hi, lo = _split_bf16(jnp.concatenate([gf_ref[0], gb_ref[0]], axis=1))
    b2 = (jnp.dot(tri_ref[...], hi, preferred_element_type=F32)
          + jnp.dot(tri_ref[...], lo, preferred_element_type=F32))
    for slot in range(2 * hp):
        b_ref[slot] = b2[:, slot * HEAD_DIM:(slot + 1) * HEAD_DIM]

    for h, cols in enumerate(lanes):
        o, st = _hgrn_direction(qf_ref[0, :, cols], kf_ref[0, :, cols], vf_ref[0, :, cols],
                                gf_ref[0, :, cols], b_ref, h, code_ref, st_ref[h], False)
        of_ref[0, :, cols] = o
        st_ref[h] = st
        o, st = _hgrn_direction(qb_ref[0, :, cols], kb_ref[0, :, cols], vb_ref[0, :, cols],
                                gb_ref[0, :, cols], b_ref, hp + h, code_ref, st_ref[hp + h], True)
        ob_ref[0, :, cols] = o
        st_ref[hp + h] = st

    @pl.when(j == pl.num_programs(2) - 1)
    def _():
        sf_ref[0] = st_ref[0:hp]
        sb_ref[0] = st_ref[hp:2 * hp]


def _hgrn_call(z, zk, s0f, s0b):
    b, l, _ = z.shape
    width = zk.shape[2] // 2
    heads = width // HEAD_DIM
    hp = min(HGRN_HEADS_PER_STEP, heads)
    groups = heads // hp
    c = min(HGRN_CHUNK, l)
    n = l // c
    col = lambda part: (lambda bi, h, j: (bi, j, part * groups + h))
    col_rev = lambda part: (lambda bi, h, j: (bi, n - 1 - j, part * groups + h))
    tok = lambda imap: pl.BlockSpec((1, c, hp * HEAD_DIM), imap)
    st_spec = pl.BlockSpec((1, hp, HEAD_DIM, HEAD_DIM), lambda bi, h, j: (bi, h, 0, 0))
    st_shape = jax.ShapeDtypeStruct((b, heads, HEAD_DIM, HEAD_DIM), F32)
    return pl.pallas_call(
        _hgrn_kernel,
        out_shape=(jax.ShapeDtypeStruct((b, l, width), F32),
                   jax.ShapeDtypeStruct((b, l, width), F32), st_shape, st_shape),
        grid=(b, groups, n),
        in_specs=[tok(col(PART_Q)), tok(col(PART_F_FWD)), tok(col(0)), tok(col(PART_V)),
                  tok(col_rev(PART_Q)), tok(col_rev(PART_F_BWD)), tok(col_rev(1)), tok(col_rev(PART_V)),
                  st_spec, st_spec],
        out_specs=(tok(lambda bi, h, j: (bi, j, h)), tok(lambda bi, h, j: (bi, n - 1 - j, h)),
                   st_spec, st_spec),
        scratch_shapes=[pltpu.VMEM((2 * hp, HEAD_DIM, HEAD_DIM), F32),
                        pltpu.VMEM((2 * hp, c, HEAD_DIM), F32),
                        pltpu.VMEM((2, c // 2, c // 2), jnp.int32),
                        pltpu.VMEM((c, c), BF16)],
        compiler_params=_params("parallel", "parallel", "arbitrary"),
        name="hgrn2_scan",
    )(z, z, zk, z, z, z, zk, z, s0f, s0b)


def _pool_project(m, v, g, wp_ref, bp_ref, ps_ref, y_ref):
    cols = slice(g * POOL_GROUP_DIM, (g + 1) * POOL_GROUP_DIM)
    y = jnp.dot((m - v).astype(BF16), wp_ref[g], preferred_element_type=F32)
    y_ref[0, :, cols] = ((y + bp_ref[:, cols]) * ps_ref[:, cols]).astype(y_ref.dtype)


def _pool2d_kernel(prev_ref, cur_ref, next_ref, wp_ref, bp_ref, ps_ref, y_ref, cs_ref, *, rows):
    i = pl.program_id(1)
    tile = cur_ref.shape[1]

    tok = lax.broadcasted_iota(jnp.int32, (tile, LANES), 0)
    grow = tok // GRID_W + i * POOL_TILE_ROWS
    gcol = tok % GRID_W
    eo = lax.broadcasted_iota(jnp.int32, (COL_BLOCK, COL_BLOCK), 0)
    ei = lax.broadcasted_iota(jnp.int32, (COL_BLOCK, COL_BLOCK), 1)
    same_row = (eo // GRID_W) == (ei // GRID_W)
    dcol = ei % GRID_W - eo % GRID_W

    for g, w in enumerate(POOL_WINDOWS):
        cols = slice(g * POOL_GROUP_DIM, (g + 1) * POOL_GROUP_DIM)
        back = w // 2
        band = jnp.where(same_row & (dcol >= -back) & (dcol < w - back), 1.0, 0.0).astype(BF16)
        pieces = (
            jnp.where(i > 0, prev_ref[0, :, cols], 0.0),
            cur_ref[0, :, cols],
            jnp.where(i < pl.num_programs(1) - 1, next_ref[0, :, cols], 0.0),
        )
        off = 0
        for piece in pieces:
            for s in range(0, piece.shape[0], COL_BLOCK):
                hi, lo = _split_bf16(piece[s:s + COL_BLOCK])
                cs_ref[off + s:off + s + COL_BLOCK, :] = (
                    jnp.dot(band, hi, preferred_element_type=F32)
                    + jnp.dot(band, lo, preferred_element_type=F32))
            off += piece.shape[0]
        start = (POOL_HALO_ROWS - back) * GRID_W
        n_tok = tile + (w - 1) * GRID_W
        acc = cs_ref[start:start + n_tok, :]
        span = 1
        while span < w:
            n_tok -= span * GRID_W
            acc = acc[:n_tok] + acc[span * GRID_W:span * GRID_W + n_tok]
            span *= 2
        cnt_r = jnp.minimum(grow + (w - back), rows) - jnp.maximum(grow - back, 0)
        cnt_c = jnp.minimum(gcol + (w - back), GRID_W) - jnp.maximum(gcol - back, 0)
        inv = 1.0 / (cnt_r * cnt_c).astype(F32)
        m = acc * jnp.concatenate([inv] * (POOL_GROUP_DIM // LANES), axis=1)
        _pool_project(m, cur_ref[0, :, cols], g, wp_ref, bp_ref, ps_ref, y_ref)


def _pool2d_call(z, w_pool, b_pool, pool_scale, rows):
    b, l, n = z.shape
    width = w_pool.shape[0] * POOL_GROUP_DIM
    p_blk = PART_POOL
    assert rows % POOL_TILE_ROWS == 0 and POOL_TILE_ROWS % POOL_HALO_ROWS == 0
    assert max(POOL_WINDOWS) // 2 <= POOL_HALO_ROWS
    tile = POOL_TILE_ROWS * GRID_W
    halo = POOL_HALO_ROWS * GRID_W
    n_halo = l // halo
    per = tile // halo
    vec = pl.BlockSpec((1, width), lambda bi, i: (0, 0))
    return pl.pallas_call(
        functools.partial(_pool2d_kernel, rows=rows),
        out_shape=jax.ShapeDtypeStruct((b, l, width), BF16),
        grid=(b, l // tile),
        in_specs=[
            pl.BlockSpec((1, halo, width), lambda bi, i: (bi, jnp.maximum(i * per - 1, 0), p_blk)),
            pl.BlockSpec((1, tile, width), lambda bi, i: (bi, i, p_blk)),
            pl.BlockSpec((1, halo, width),
                         lambda bi, i: (bi, jnp.minimum((i + 1) * per, n_halo - 1), p_blk)),
            pl.BlockSpec(w_pool.shape, lambda bi, i: (0, 0, 0)),
            vec, vec,
        ],
        out_specs=pl.BlockSpec((1, tile, width), lambda bi, i: (bi, i, 0)),
        scratch_shapes=[pltpu.VMEM((halo + tile + halo, POOL_GROUP_DIM), F32)],
        compiler_params=_params("parallel", "parallel"),
        name="pool_mixer_2d",
    )(z, z, z, w_pool, b_pool, pool_scale)


def _pool1d_kernel(p_ref, wp_ref, bp_ref, ps_ref, y_ref):
    l = p_ref.shape[1]
    to = lax.broadcasted_iota(jnp.int32, (l, l), 0)
    dt = lax.broadcasted_iota(jnp.int32, (l, l), 1) - to
    pos = lax.broadcasted_iota(jnp.int32, (l, POOL_GROUP_DIM), 0)
    for g, w in enumerate(POOL_WINDOWS):
        cols = slice(g * POOL_GROUP_DIM, (g + 1) * POOL_GROUP_DIM)
        back = w // 2
        band = jnp.where((dt >= -back) & (dt < w - back), 1.0, 0.0).astype(BF16)
        v = p_ref[0, :, cols]
        hi, lo = _split_bf16(v)
        total = (jnp.dot(band, hi, preferred_element_type=F32)
                 + jnp.dot(band, lo, preferred_element_type=F32))
        cnt = jnp.minimum(pos + (w - back), l) - jnp.maximum(pos - back, 0)
        _pool_project(total / cnt.astype(F32), v, g, wp_ref, bp_ref, ps_ref, y_ref)


def _pool1d_call(z, w_pool, b_pool, pool_scale):
    b, l, n = z.shape
    width = w_pool.shape[0] * POOL_GROUP_DIM
    p_blk = PART_POOL
    vec = pl.BlockSpec((1, width), lambda bi: (0, 0))
    return pl.pallas_call(
        _pool1d_kernel,
        out_shape=jax.ShapeDtypeStruct((b, l, width), BF16),
        grid=(b,),
        in_specs=[pl.BlockSpec((1, l, width), lambda bi: (bi, 0, p_blk)),
                  pl.BlockSpec(w_pool.shape, lambda bi: (0, 0, 0)), vec, vec],
        out_specs=pl.BlockSpec((1, l, width), lambda bi: (bi, 0, 0)),
        compiler_params=_params("parallel"),
        name="pool_mixer_1d",
    )(z, w_pool, b_pool, pool_scale)


def _outproj_kernel(of_ref, ob_ref, gate_ref, y_ref, x_ref, mod_ref, g_ref, hg_ref, w_ref, o_ref):
    width = of_ref.shape[2]
    o = of_ref[0] + ob_ref[0]
    heads = []
    for h in range(width // HEAD_DIM):
        heads.append(_rms(o[:, h * HEAD_DIM:(h + 1) * HEAD_DIM]) * hg_ref[...])
    og = (jnp.concatenate(heads, axis=-1) * gate_ref[0]).astype(BF16)
    mix = jnp.dot(jnp.concatenate([og, y_ref[0]], axis=-1), w_ref[...], preferred_element_type=F32)
    r = _rms(mix) * g_ref[pl.ds(3, 1), :]
    o_ref[0] = x_ref[0] + mod_ref[0, pl.ds(5, 1), :] * r


def _outproj_call(o_fw, o_bw, z, y, x, mod, gains, hgrn_gain, w, layer):
    b, l, d = x.shape
    width = o_fw.shape[2]
    g_blk = PART_GATE
    tm = min(512, l)
    tokw = lambda imap: pl.BlockSpec((1, tm, width), imap)
    row = lambda bi, i: (bi, i, 0)
    return pl.pallas_call(
        _outproj_kernel,
        out_shape=jax.ShapeDtypeStruct((b, l, d), F32),
        grid=(b, l // tm),
        in_specs=[
            tokw(row), tokw(row), tokw(lambda bi, i: (bi, i, g_blk)), tokw(row),
            pl.BlockSpec((1, tm, d), row),
            pl.BlockSpec((1, N_MOD, d), lambda bi, i: (bi, 0, 0)),
            pl.BlockSpec(gains.shape, lambda bi, i: (0, 0)),
            pl.BlockSpec((1, HEAD_DIM), lambda bi, i: (0, 0)),
            pl.BlockSpec((None,) + w.shape[1:], lambda bi, i: (layer, 0, 0)),
        ],
        out_specs=pl.BlockSpec((1, tm, d), row),
        compiler_params=_params("parallel", "parallel"),
        name="mixer_out_proj",
    )(o_fw, o_bw, z, y, x, mod, gains, hgrn_gain, w)


def kernel(x, c, ctx, c_ctx, w_ada, b_ada, norm_gain, ffn_in, ffn_out, w_in, hgrn_lb, hgrn_gain,
           w_pool, b_pool, pool_scale, w_out):
    bsz, seq, d = x.shape
    depth = w_ada.shape[0]
    rows = seq // GRID_W
    width = hgrn_lb.shape[-1]
    heads = width // HEAD_DIM

    lbp = jax.nn.softmax(hgrn_lb.astype(F32), axis=0)
    lower = jnp.cumsum(lbp, axis=0) - lbp[0]
    one_m_lb = 1.0 - lower

    cvec = jnp.zeros((MOD_ROWS, d), F32).at[:bsz].set(c).at[bsz].set(c_ctx)
    mod_all = _mod_call(cvec, w_ada, b_ada)

    ffn_in_b, ffn_out_b = ffn_in.astype(BF16), ffn_out.astype(BF16)
    w_in_b, w_out_b, w_pool_b = w_in.astype(BF16), w_out.astype(BF16), w_pool.astype(BF16)

    s0 = jnp.zeros((bsz, heads, HEAD_DIM, HEAD_DIM), F32)
    ctx_len = ctx.shape[1]
    ctx = ctx.reshape(1, bsz * ctx_len, d)
    per_sample = lambda a: a.reshape(bsz, ctx_len, a.shape[-1])
    flat = lambda a: a.reshape(1, bsz * ctx_len, a.shape[-1])
    for l in range(depth):
        last = l == depth - 1
        gains = norm_gain[l]
        mod_x = mod_all[l, :bsz].reshape(bsz, N_MOD, d)
        mod_c = mod_all[l, bsz].reshape(1, N_MOD, d)
        hg = hgrn_gain[l].reshape(1, HEAD_DIM)
        bp = b_pool[l].reshape(1, -1)
        ps = pool_scale[l].reshape(1, -1)

        x = _ffn_call(x, mod_x, gains, ffn_in_b, ffn_out_b, l, 0)
        ctx = _ffn_call(ctx, mod_c, gains, ffn_in_b, ffn_out_b, l, 0)

        z_x, zk_x = _inproj_call(x, mod_x, gains, w_in_b, l, lower[l], one_m_lb[l])
        z_c, zk_c = _inproj_call(ctx, mod_c, gains, w_in_b, l, lower[l], one_m_lb[l])
        oc_fw, oc_bw, s_fw, s_bw = _hgrn_call(per_sample(z_c), per_sample(zk_c), s0, s0)
        ox_fw, ox_bw, _, _ = _hgrn_call(z_x, zk_x, s_fw, s_bw)
        y_x = _pool2d_call(z_x, w_pool_b[l], bp, ps, rows)
        x = _outproj_call(ox_fw, ox_bw, z_x, y_x, x, mod_x, gains, hg, w_out_b, l)
        if not last:
            y_c = _pool1d_call(per_sample(z_c), w_pool_b[l], bp, ps)
            ctx = _outproj_call(flat(oc_fw), flat(oc_bw), z_c, flat(y_c), ctx, mod_c, gains, hg, w_out_b, l)
            ctx = _ffn_call(ctx, mod_c, gains, ffn_in_b, ffn_out_b, l, 1)

        x = _ffn_call(x, mod_x, gains, ffn_in_b, ffn_out_b, l, 1)
    return x
```

```python
import functools

import jax
import jax.numpy as jnp
from jax import lax
from jax.experimental import pallas as pl
from jax.experimental.pallas import tpu as pltpu

F32 = jnp.float32
BF16 = jnp.bfloat16

HEAD_DIM = 128
GRID_W = 64
POOL_WINDOWS = (2, 4, 8, 16)
POOL_GROUP_DIM = 256
N_MOD = 9
EPS = 1e-6
LOG2_E = 1.4426950408889634
PARTS = PART_Q, PART_F_FWD, PART_F_BWD, PART_V, PART_GATE, PART_POOL = tuple(range(6))

LANES = 128
V7X_VMEM_LIMIT_BYTES = 56 * 1024 * 1024
TOKEN_ROWS = 1024
FFN_HIDDEN_COLS = 512
OUTPROJ_ROWS = 512
EDGE_SUB_ROWS = 256
MOD_ROWS = 8
MOD_COLS = 1024
HGRN_CHUNK = 256
HGRN_HEADS_PER_STEP = 8
POOL_TILE_ROWS = 16
POOL_HALO_ROWS = 8
COL_BLOCK = 256


def _params(*semantics):
    return pltpu.CompilerParams(dimension_semantics=semantics,
                                vmem_limit_bytes=V7X_VMEM_LIMIT_BYTES)


def _sigmoid(x):
    return 1.0 / (1.0 + jnp.exp(-x))


def _neg_abs(d):
    bits = lax.bitcast_convert_type(d, jnp.uint32) | jnp.uint32(0x80000000)
    return lax.bitcast_convert_type(bits, F32)


def _rms(x):
    return x * lax.rsqrt(jnp.mean(x * x, axis=-1, keepdims=True) + EPS)


def _split_bf16(v):
    hi = v.astype(BF16)
    lo = (v - hi.astype(F32)).astype(BF16)
    return hi, lo


def _mod_kernel(c_ref, w_ref, b_ref, o_ref):
    c = c_ref[...]
    s = (c * _sigmoid(c)).astype(BF16)
    o_ref[0] = jnp.dot(s, w_ref[0].astype(BF16), preferred_element_type=F32) + b_ref[0]


def _mod_call(cvec, w_ada, b_ada):
    depth, d, n = w_ada.shape
    tn = MOD_COLS
    return pl.pallas_call(
        _mod_kernel,
        out_shape=jax.ShapeDtypeStruct((depth, MOD_ROWS, n), F32),
        grid=(depth, n // tn),
        in_specs=[
            pl.BlockSpec((MOD_ROWS, d), lambda l, i: (0, 0)),
            pl.BlockSpec((1, d, tn), lambda l, i: (l, 0, i)),
            pl.BlockSpec((1, 1, tn), lambda l, i: (l, 0, i)),
        ],
        out_specs=pl.BlockSpec((1, MOD_ROWS, tn), lambda l, i: (l, 0, i)),
        compiler_params=_params("parallel", "parallel"),
        name="adaln_mod",
    )(cvec, w_ada, b_ada.reshape(depth, 1, n))


def _lane_blocks(d):
    return [slice(s, s + LANES) for s in range(0, d, LANES)]


def _row_tiles(n_rows):
    sub = min(EDGE_SUB_ROWS, n_rows)
    return [slice(r, r + sub) for r in range(0, n_rows, sub)]


def _one_behind(items, produce, consume):
    pending = None
    for item in items:
        made = produce(item)
        if pending is not None:
            consume(*pending)
        pending = (item, made)
    consume(*pending)


def _row_scale(src_ref, rows):
    d = src_ref.shape[2]
    acc = None
    for cols in _lane_blocks(d):
        xb = src_ref[0, rows, cols]
        acc = xb * xb if acc is None else acc + xb * xb
    ss = jnp.sum(acc, axis=-1, keepdims=True)
    return jnp.broadcast_to(lax.rsqrt(ss * (1.0 / d) + EPS), acc.shape)


def _modulated_norm(x_ref, rows, gain, shift, scale, h_ref):
    rs = _row_scale(x_ref, rows)
    for cols in _lane_blocks(x_ref.shape[2]):
        mul = gain[:, cols] * (1.0 + scale[:, cols])
        h_ref[rows, cols] = (x_ref[0, rows, cols] * rs * mul + shift[:, cols]).astype(BF16)


def _ffn_kernel(x_ref, mod_ref, g_ref, wg_ref, wu_ref, wo_ref, o_ref, h_ref, *, j):
    f = pl.program_id(2)
    last = pl.num_programs(2) - 1
    tm, d = h_ref.shape

    def swiglu(rows):
        h = h_ref[rows, :]
        gate = jnp.dot(h, wg_ref[...], preferred_element_type=F32)
        up = jnp.dot(h, wu_ref[...], preferred_element_type=F32)
        a = (gate * _sigmoid(gate) * up).astype(BF16)
        return jnp.dot(a, wo_ref[...], preferred_element_type=F32)

    @pl.when(f == 0)
    def _():
        gain = g_ref[pl.ds(2 * j, 1), :]
        shift = mod_ref[0, pl.ds(3 * j, 1), :]
        scale = mod_ref[0, pl.ds(3 * j + 1, 1), :]
        for rows in _row_tiles(tm):
            _modulated_norm(x_ref, rows, gain, shift, scale, h_ref)
            o_ref[0, rows, :] = swiglu(rows)

    @pl.when((f > 0) & (f < last))
    def _():
        o_ref[0] += swiglu(slice(None))

    @pl.when(f == last)
    def _():
        gain = g_ref[pl.ds(2 * j + 1, 1), :]
        half_gate = 0.5 * mod_ref[0, pl.ds(3 * j + 2, 1), :]
        for rows in _row_tiles(tm):
            o_ref[0, rows, :] += swiglu(rows)
            rs = _row_scale(o_ref, rows)
            for cols in _lane_blocks(d):
                mul = half_gate[:, cols] * gain[:, cols]
                o_ref[0, rows, cols] = x_ref[0, rows, cols] + o_ref[0, rows, cols] * rs * mul


def _ffn_call(x, mod, gains, w_in, w_out, layer, which):
    b, l, d = x.shape
    d_ff = w_out.shape[2]
    tm = min(TOKEN_ROWS, l)
    tf = FFN_HIDDEN_COLS
    nf = d_ff // tf
    assert nf >= 2, "the kernel's first and last grid steps are distinct code paths"
    return pl.pallas_call(
        functools.partial(_ffn_kernel, j=2 * which),
        out_shape=jax.ShapeDtypeStruct((b, l, d), F32),
        grid=(b, l // tm, nf),
        in_specs=[
            pl.BlockSpec((1, tm, d), lambda bi, i, f: (bi, i, 0)),
            pl.BlockSpec((1, N_MOD, d), lambda bi, i, f: (bi, 0, 0)),
            pl.BlockSpec(gains.shape, lambda bi, i, f: (0, 0)),
            pl.BlockSpec((None, None, d, tf), lambda bi, i, f: (layer, which, 0, f)),
            pl.BlockSpec((None, None, d, tf), lambda bi, i, f: (layer, which, 0, f + nf)),
            pl.BlockSpec((None, None, tf, d), lambda bi, i, f: (layer, which, f, 0)),
        ],
        out_specs=pl.BlockSpec((1, tm, d), lambda bi, i, f: (bi, i, 0)),
        scratch_shapes=[pltpu.VMEM((tm, d), BF16)],
        compiler_params=_params("parallel", "parallel", "arbitrary"),
        name="ffn_sublayer",
    )(x, mod, gains, w_in, w_in, w_out)


def _log2_forget(z, lb):
    z2 = z * LOG2_E
    e = jnp.exp2(_neg_abs(z2))
    pos = z > 0
    num = jnp.where(pos, 1.0, lb) + jnp.where(pos, lb * e, e)
    log2_num = jnp.where((lb == 0.0) & jnp.logical_not(pos), z2, jnp.log2(num))
    return log2_num - jnp.log2(1.0 + e)


def _inproj_kernel(x_ref, mod_ref, g_ref, w_ref, lb_ref, z_ref, h_ref):
    part = pl.program_id(2)
    row_tiles = _row_tiles(h_ref.shape[0])

    def project(rows):
        return jnp.dot(h_ref[rows, :], w_ref[...], preferred_element_type=F32)

    def store_silu(rows, a):
        z_ref[0, rows, :] = a * _sigmoid(a)

    @pl.when(part == PART_Q)
    def _():
        gain, shift, scale = g_ref[pl.ds(2, 1), :], mod_ref[0, pl.ds(3, 1), :], mod_ref[0, pl.ds(4, 1), :]

        def norm_project(rows):
            _modulated_norm(x_ref, rows, gain, shift, scale, h_ref)
            return project(rows)

        _one_behind(row_tiles, norm_project, store_silu)

    @pl.when(part == PART_GATE)
    def _():
        _one_behind(row_tiles, project, store_silu)

    for direction, forget_part in enumerate((PART_F_FWD, PART_F_BWD)):
        @pl.when(part == forget_part)
        def _(direction=direction):
            def store_log2_forget(rows, a):
                z_ref[0, rows, :] = _log2_forget(a, lb_ref[direction:direction + 1, :])

            _one_behind(row_tiles, project, store_log2_forget)

    @pl.when((part == PART_V) | (part == PART_POOL))
    def _():
        z_ref[0] = project(slice(None))


def _inproj_call(x, mod, gains, w, layer, lb):
    b, l, d = x.shape
    n = w.shape[2]
    width = lb.shape[1]
    assert n == len(PARTS) * width
    tm = min(TOKEN_ROWS, l)
    return pl.pallas_call(
        _inproj_kernel,
        out_shape=jax.ShapeDtypeStruct((b, l, n), F32),
        grid=(b, l // tm, len(PARTS)),
        in_specs=[
            pl.BlockSpec((1, tm, d), lambda bi, i, k: (bi, i, 0)),
            pl.BlockSpec((1, N_MOD, d), lambda bi, i, k: (bi, 0, 0)),
            pl.BlockSpec(gains.shape, lambda bi, i, k: (0, 0)),
            pl.BlockSpec((None, d, width), lambda bi, i, k: (layer, 0, k)),
            pl.BlockSpec(lb.shape, lambda bi, i, k: (0, 0)),
        ],
        out_specs=pl.BlockSpec((1, tm, width), lambda bi, i, k: (bi, i, k)),
        scratch_shapes=[pltpu.VMEM((tm, d), BF16)],
        compiler_params=_params("parallel", "parallel", "arbitrary"),
        name="mixer_in_proj",
    )(x, mod, gains, w, lb)


def _mid_rows(b_ref, slot, c, half):
    if half >= 4:
        blk = 2 * half
        return jnp.concatenate(
            [jnp.broadcast_to(b_ref[slot, pl.ds(s + half - 1, 1), :], (blk, HEAD_DIM))
             for s in range(0, c, blk)], axis=0)
    sub = lax.broadcasted_iota(jnp.int32, (8, HEAD_DIM), 0)
    vregs = []
    for s in range(0, c, 8):
        lo = jnp.broadcast_to(b_ref[slot, pl.ds(s + 1, 1), :], (8, HEAD_DIM))
        hi = jnp.broadcast_to(b_ref[slot, pl.ds(s + 5, 1), :], (8, HEAD_DIM))
        vregs.append(jnp.where(sub < 4, lo, hi))
    return jnp.concatenate(vregs, axis=0)


def _pick_halves(even_src, odd_src, half, odd_mask):
    c = even_src.shape[0]
    if half < 8:
        return jnp.where(odd_mask, odd_src, even_src)
    shp = (c // (2 * half), 2, half, HEAD_DIM)
    both = jnp.stack([even_src.reshape(shp)[:, 0], odd_src.reshape(shp)[:, 1]], axis=1)
    return both.reshape(c, HEAD_DIM)


def _hgrn_direction(q, v, g, b_ref, slot, code_ref, st, reverse):
    c = q.shape[0]
    hc = c // 2
    k = 1.0 - jnp.exp2(g)
    b = b_ref[slot]
    beta = b - g if reverse else b
    row = lax.broadcasted_iota(jnp.int32, (c, HEAD_DIM), 0)
    def level_products(half):
        odd = (row & half) != 0 if half < 8 else None
        if half == 1:
            u = jnp.where(odd, 0.0, g) if reverse else jnp.where(odd, g, 0.0)
        else:
            u = _neg_abs(beta - _mid_rows(b_ref, slot, c, half))
        src = _pick_halves(q, k, half, odd) if reverse else _pick_halves(k, q, half, odd)
        r = (src * jnp.exp2(u)).astype(BF16)
        return lax.dot_general(r, r, (((1,), (1,)), ((), ())), preferred_element_type=F32)

    first = jnp.zeros((hc, hc), F32)
    second = jnp.zeros((hc, hc), F32)
    half = 1
    while half < hc:
        a = level_products(half)
        mask = code_ref[int(reverse)] >= half
        first = jnp.where(mask, a[:hc, :hc], first)
        second = jnp.where(mask, a[hc:, hc:], second)
        half *= 2
    a = level_products(hc)

    tot = b_ref[slot, pl.ds(c - 1, 1), :]
    q_in, k_out = (tot - beta, beta) if reverse else (b, tot - b)
    q_state = (q * jnp.exp2(q_in)).astype(BF16)
    k_state = (k * jnp.exp2(k_out)).astype(BF16)
    vb = v.astype(BF16)
    if reverse:
        top = jnp.concatenate([first, a[:hc, hc:]], axis=1).astype(BF16)
        o = jnp.concatenate([jnp.dot(top, vb, preferred_element_type=F32),
                             jnp.dot(second.astype(BF16), vb[hc:], preferred_element_type=F32)], axis=0)
    else:
        bottom = jnp.concatenate([a[hc:, :hc], second], axis=1).astype(BF16)
        o = jnp.concatenate([jnp.dot(first.astype(BF16), vb[:hc], preferred_element_type=F32),
                             jnp.dot(bottom, vb, preferred_element_type=F32)], axis=0)
    o += lax.dot_general(q_state, st.astype(BF16), (((1,), (1,)), ((), ())),
                         preferred_element_type=F32)
    o += jnp.sum(q * k, axis=-1, keepdims=True) * v
    st_new = st * jnp.exp2(tot) + lax.dot_general(
        vb, k_state, (((0,), (0,)), ((), ())), preferred_element_type=F32)
    return o, st_new


def _hgrn_kernel(qf_ref, gf_ref, vf_ref, qb_ref, gb_ref, vb_ref,
                 s0f_ref, s0b_ref, of_ref, ob_ref, sf_ref, sb_ref, st_ref, b_ref, code_ref, tri_ref):
    j = pl.program_id(2)
    c = qf_ref.shape[1]
    hp = qf_ref.shape[2] // HEAD_DIM
    lanes = [slice(h * HEAD_DIM, (h + 1) * HEAD_DIM) for h in range(hp)]

    @pl.when(j == 0)
    def _():
        st_ref[0:hp] = s0f_ref[0]
        st_ref[hp:2 * hp] = s0b_ref[0]
        ti = lax.broadcasted_iota(jnp.int32, (c, c), 0)
        si = lax.broadcasted_iota(jnp.int32, (c, c), 1)
        tri_ref[...] = jnp.where(ti >= si, 1.0, 0.0).astype(BF16)
        ti = lax.broadcasted_iota(jnp.int32, (c // 2, c // 2), 0)
        si = lax.broadcasted_iota(jnp.int32, (c // 2, c // 2), 1)
        level = ti ^ si
        code_ref[0] = jnp.where(ti > si, level, 0)
        code_ref[1] = jnp.where(ti < si, level, 0)

    hi, lo = _split_bf16(jnp.concatenate([gf_ref[0], gb_ref[0]], axis=1))
    b2 = (jnp.dot(tri_ref[...], hi, preferred_element_type=F32)
          + jnp.dot(tri_ref[...], lo, preferred_element_type=F32))
    for slot in range(2 * hp):
        b_ref[slot] = b2[:, slot * HEAD_DIM:(slot + 1) * HEAD_DIM]

    for h, cols in enumerate(lanes):
        o, st = _hgrn_direction(qf_ref[0, :, cols], vf_ref[0, :, cols], gf_ref[0, :, cols],
                                b_ref, h, code_ref, st_ref[h], False)
        of_ref[0, :, cols] = o
        st_ref[h] = st
        o, st = _hgrn_direction(qb_ref[0, :, cols], vb_ref[0, :, cols], gb_ref[0, :, cols],
                                b_ref, hp + h, code_ref, st_ref[hp + h], True)
        ob_ref[0, :, cols] = o
        st_ref[hp + h] = st

    @pl.when(j == pl.num_programs(2) - 1)
    def _():
        sf_ref[0] = st_ref[0:hp]
        sb_ref[0] = st_ref[hp:2 * hp]


def _hgrn_call(z, s0f, s0b):
    b, l, n = z.shape
    width = n // len(PARTS)
    heads = width // HEAD_DIM
    hp = min(HGRN_HEADS_PER_STEP, heads)
    groups = heads // hp
    c = min(HGRN_CHUNK, l)
    n = l // c
    col = lambda part: (lambda bi, h, j: (bi, j, part * groups + h))
    col_rev = lambda part: (lambda bi, h, j: (bi, n - 1 - j, part * groups + h))
    tok = lambda imap: pl.BlockSpec((1, c, hp * HEAD_DIM), imap)
    st_spec = pl.BlockSpec((1, hp, HEAD_DIM, HEAD_DIM), lambda bi, h, j: (bi, h, 0, 0))
    st_shape = jax.ShapeDtypeStruct((b, heads, HEAD_DIM, HEAD_DIM), F32)
    return pl.pallas_call(
        _hgrn_kernel,
        out_shape=(jax.ShapeDtypeStruct((b, l, width), F32),
                   jax.ShapeDtypeStruct((b, l, width), F32), st_shape, st_shape),
        grid=(b, groups, n),
        in_specs=[tok(col(PART_Q)), tok(col(PART_F_FWD)), tok(col(PART_V)),
                  tok(col_rev(PART_Q)), tok(col_rev(PART_F_BWD)), tok(col_rev(PART_V)),
                  st_spec, st_spec],
        out_specs=(tok(lambda bi, h, j: (bi, j, h)), tok(lambda bi, h, j: (bi, n - 1 - j, h)),
                   st_spec, st_spec),
        scratch_shapes=[pltpu.VMEM((2 * hp, HEAD_DIM, HEAD_DIM), F32),
                        pltpu.VMEM((2 * hp, c, HEAD_DIM), F32),
                        pltpu.VMEM((2, c // 2, c // 2), jnp.int32),
                        pltpu.VMEM((c, c), BF16)],
        compiler_params=_params("parallel", "parallel", "arbitrary"),
        name="hgrn2_scan",
    )(z, z, z, z, z, z, s0f, s0b)


def _pool_project(m, v, g, wp_ref, bp_ref, ps_ref, y_ref):
    cols = slice(g * POOL_GROUP_DIM, (g + 1) * POOL_GROUP_DIM)
    y = jnp.dot((m - v).astype(BF16), wp_ref[g], preferred_element_type=F32)
    y_ref[0, :, cols] = ((y + bp_ref[:, cols]) * ps_ref[:, cols]).astype(y_ref.dtype)


def _pool2d_kernel(prev_ref, cur_ref, next_ref, wp_ref, bp_ref, ps_ref, y_ref, cs_ref, *, rows):
    i = pl.program_id(1)
    tile = cur_ref.shape[1]

    tok = lax.broadcasted_iota(jnp.int32, (tile, LANES), 0)
    grow = tok // GRID_W + i * POOL_TILE_ROWS
    gcol = tok % GRID_W
    eo = lax.broadcasted_iota(jnp.int32, (COL_BLOCK, COL_BLOCK), 0)
    ei = lax.broadcasted_iota(jnp.int32, (COL_BLOCK, COL_BLOCK), 1)
    same_row = (eo // GRID_W) == (ei // GRID_W)
    dcol = ei % GRID_W - eo % GRID_W

    for g, w in enumerate(POOL_WINDOWS):
        cols = slice(g * POOL_GROUP_DIM, (g + 1) * POOL_GROUP_DIM)
        back = w // 2
        band = jnp.where(same_row & (dcol >= -back) & (dcol < w - back), 1.0, 0.0).astype(BF16)
        pieces = (
            jnp.where(i > 0, prev_ref[0, :, cols], 0.0),
            cur_ref[0, :, cols],
            jnp.where(i < pl.num_programs(1) - 1, next_ref[0, :, cols], 0.0),
        )
        off = 0
        for piece in pieces:
            for s in range(0, piece.shape[0], COL_BLOCK):
                hi, lo = _split_bf16(piece[s:s + COL_BLOCK])
                cs_ref[off + s:off + s + COL_BLOCK, :] = (
                    jnp.dot(band, hi, preferred_element_type=F32)
                    + jnp.dot(band, lo, preferred_element_type=F32))
            off += piece.shape[0]
        start = (POOL_HALO_ROWS - back) * GRID_W
        n_tok = tile + (w - 1) * GRID_W
        acc = cs_ref[start:start + n_tok, :]
        span = 1
        while span < w:
            n_tok -= span * GRID_W
            acc = acc[:n_tok] + acc[span * GRID_W:span * GRID_W + n_tok]
            span *= 2
        cnt_r = jnp.minimum(grow + (w - back), rows) - jnp.maximum(grow - back, 0)
        cnt_c = jnp.minimum(gcol + (w - back), GRID_W) - jnp.maximum(gcol - back, 0)
        inv = 1.0 / (cnt_r * cnt_c).astype(F32)
        m = acc * jnp.concatenate([inv] * (POOL_GROUP_DIM // LANES), axis=1)
        _pool_project(m, cur_ref[0, :, cols], g, wp_ref, bp_ref, ps_ref, y_ref)


def _pool2d_call(z, w_pool, b_pool, pool_scale, rows):
    b, l, n = z.shape
    width = w_pool.shape[0] * POOL_GROUP_DIM
    p_blk = PART_POOL
    assert rows % POOL_TILE_ROWS == 0 and POOL_TILE_ROWS % POOL_HALO_ROWS == 0
    assert max(POOL_WINDOWS) // 2 <= POOL_HALO_ROWS
    tile = POOL_TILE_ROWS * GRID_W
    halo = POOL_HALO_ROWS * GRID_W
    n_halo = l // halo
    per = tile // halo
    vec = pl.BlockSpec((1, width), lambda bi, i: (0, 0))
    return pl.pallas_call(
        functools.partial(_pool2d_kernel, rows=rows),
        out_shape=jax.ShapeDtypeStruct((b, l, width), BF16),
        grid=(b, l // tile),
        in_specs=[
            pl.BlockSpec((1, halo, width), lambda bi, i: (bi, jnp.maximum(i * per - 1, 0), p_blk)),
            pl.BlockSpec((1, tile, width), lambda bi, i: (bi, i, p_blk)),
            pl.BlockSpec((1, halo, width),
                         lambda bi, i: (bi, jnp.minimum((i + 1) * per, n_halo - 1), p_blk)),
            pl.BlockSpec(w_pool.shape, lambda bi, i: (0, 0, 0)),
            vec, vec,
        ],
        out_specs=pl.BlockSpec((1, tile, width), lambda bi, i: (bi, i, 0)),
        scratch_shapes=[pltpu.VMEM((halo + tile + halo, POOL_GROUP_DIM), F32)],
        compiler_params=_params("parallel", "parallel"),
        name="pool_mixer_2d",
    )(z, z, z, w_pool, b_pool, pool_scale)


def _pool1d_kernel(p_ref, wp_ref, bp_ref, ps_ref, y_ref):
    l = p_ref.shape[1]
    to = lax.broadcasted_iota(jnp.int32, (l, l), 0)
    dt = lax.broadcasted_iota(jnp.int32, (l, l), 1) - to
    pos = lax.broadcasted_iota(jnp.int32, (l, POOL_GROUP_DIM), 0)
    for g, w in enumerate(POOL_WINDOWS):
        cols = slice(g * POOL_GROUP_DIM, (g + 1) * POOL_GROUP_DIM)
        back = w // 2
        band = jnp.where((dt >= -back) & (dt < w - back), 1.0, 0.0).astype(BF16)
        v = p_ref[0, :, cols]
        hi, lo = _split_bf16(v)
        total = (jnp.dot(band, hi, preferred_element_type=F32)
                 + jnp.dot(band, lo, preferred_element_type=F32))
        cnt = jnp.minimum(pos + (w - back), l) - jnp.maximum(pos - back, 0)
        _pool_project(total / cnt.astype(F32), v, g, wp_ref, bp_ref, ps_ref, y_ref)


def _pool1d_call(z, w_pool, b_pool, pool_scale):
    b, l, n = z.shape
    width = w_pool.shape[0] * POOL_GROUP_DIM
    p_blk = PART_POOL
    vec = pl.BlockSpec((1, width), lambda bi: (0, 0))
    return pl.pallas_call(
        _pool1d_kernel,
        out_shape=jax.ShapeDtypeStruct((b, l, width), BF16),
        grid=(b,),
        in_specs=[pl.BlockSpec((1, l, width), lambda bi: (bi, 0, p_blk)),
                  pl.BlockSpec(w_pool.shape, lambda bi: (0, 0, 0)), vec, vec],
        out_specs=pl.BlockSpec((1, l, width), lambda bi: (bi, 0, 0)),
        compiler_params=_params("parallel"),
        name="pool_mixer_1d",
    )(z, w_pool, b_pool, pool_scale)


def _outproj_kernel(of_ref, ob_ref, gate_ref, y_ref, x_ref, mod_ref, g_ref, hg_ref, w_ref, o_ref):
    width = of_ref.shape[2]
    o = of_ref[0] + ob_ref[0]
    heads = []
    for h in range(width // HEAD_DIM):
        heads.append(_rms(o[:, h * HEAD_DIM:(h + 1) * HEAD_DIM]) * hg_ref[...])
    og = (jnp.concatenate(heads, axis=-1) * gate_ref[0]).astype(BF16)
    mix = jnp.dot(jnp.concatenate([og, y_ref[0]], axis=-1), w_ref[...], preferred_element_type=F32)
    r = _rms(mix) * g_ref[pl.ds(3, 1), :]
    o_ref[0] = x_ref[0] + mod_ref[0, pl.ds(5, 1), :] * r


def _outproj_call(o_fw, o_bw, z, y, x, mod, gains, hgrn_gain, w, layer):
    b, l, d = x.shape
    width = o_fw.shape[2]
    g_blk = PART_GATE
    tm = min(OUTPROJ_ROWS, l)
    tokw = lambda imap: pl.BlockSpec((1, tm, width), imap)
    row = lambda bi, i: (bi, i, 0)
    return pl.pallas_call(
        _outproj_kernel,
        out_shape=jax.ShapeDtypeStruct((b, l, d), F32),
        grid=(b, l // tm),
        in_specs=[
            tokw(row), tokw(row), tokw(lambda bi, i: (bi, i, g_blk)), tokw(row),
            pl.BlockSpec((1, tm, d), row),
            pl.BlockSpec((1, N_MOD, d), lambda bi, i: (bi, 0, 0)),
            pl.BlockSpec(gains.shape, lambda bi, i: (0, 0)),
            pl.BlockSpec((1, HEAD_DIM), lambda bi, i: (0, 0)),
            pl.BlockSpec((None,) + w.shape[1:], lambda bi, i: (layer, 0, 0)),
        ],
        out_specs=pl.BlockSpec((1, tm, d), row),
        compiler_params=_params("parallel", "parallel"),
        name="mixer_out_proj",
    )(o_fw, o_bw, z, y, x, mod, gains, hgrn_gain, w)


def kernel(x, c, ctx, c_ctx, w_ada, b_ada, norm_gain, ffn_in, ffn_out, w_in, hgrn_lb, hgrn_gain,
           w_pool, b_pool, pool_scale, w_out):
    bsz, seq, d = x.shape
    depth = w_ada.shape[0]
    rows = seq // GRID_W
    width = hgrn_lb.shape[-1]
    heads = width // HEAD_DIM

    lbp = jax.nn.softmax(hgrn_lb.astype(F32), axis=0)
    lower = jnp.cumsum(lbp, axis=0) - lbp[0]

    cvec = jnp.zeros((MOD_ROWS, d), F32).at[:bsz].set(c).at[bsz].set(c_ctx)
    mod_all = _mod_call(cvec, w_ada, b_ada)

    ffn_in_b, ffn_out_b = ffn_in.astype(BF16), ffn_out.astype(BF16)
    w_in_b, w_out_b, w_pool_b = w_in.astype(BF16), w_out.astype(BF16), w_pool.astype(BF16)

    s0 = jnp.zeros((bsz, heads, HEAD_DIM, HEAD_DIM), F32)
    ctx_len = ctx.shape[1]
    ctx = ctx.reshape(1, bsz * ctx_len, d)
    per_sample = lambda a: a.reshape(bsz, ctx_len, a.shape[-1])
    flat = lambda a: a.reshape(1, bsz * ctx_len, a.shape[-1])
    for l in range(depth):
        last = l == depth - 1
        gains = norm_gain[l]
        mod_x = mod_all[l, :bsz].reshape(bsz, N_MOD, d)
        mod_c = mod_all[l, bsz].reshape(1, N_MOD, d)
        hg = hgrn_gain[l].reshape(1, HEAD_DIM)
        bp = b_pool[l].reshape(1, -1)
        ps = pool_scale[l].reshape(1, -1)

        x = _ffn_call(x, mod_x, gains, ffn_in_b, ffn_out_b, l, 0)
        ctx = _ffn_call(ctx, mod_c, gains, ffn_in_b, ffn_out_b, l, 0)

        z_x = _inproj_call(x, mod_x, gains, w_in_b, l, lower[l])
        z_c = _inproj_call(ctx, mod_c, gains, w_in_b, l, lower[l])
        oc_fw, oc_bw, s_fw, s_bw = _hgrn_call(per_sample(z_c), s0, s0)
        ox_fw, ox_bw, _, _ = _hgrn_call(z_x, s_fw, s_bw)
        y_x = _pool2d_call(z_x, w_pool_b[l], bp, ps, rows)
        x = _outproj_call(ox_fw, ox_bw, z_x, y_x, x, mod_x, gains, hg, w_out_b, l)
        if not last:
            y_c = _pool1d_call(per_sample(z_c), w_pool_b[l], bp, ps)
            ctx = _outproj_call(flat(oc_fw), flat(oc_bw), z_c, flat(y_c), ctx, mod_c, gains, hg, w_out_b, l)
            ctx = _ffn_call(ctx, mod_c, gains, ffn_in_b, ffn_out_b, l, 1)

        x = _ffn_call(x, mod_x, gains, ffn_in_b, ffn_out_b, l, 1)
    return x
```

```python
import functools

import jax
import jax.numpy as jnp
from jax import lax
from jax.experimental import pallas as pl
from jax.experimental.pallas import tpu as pltpu

F32 = jnp.float32
BF16 = jnp.bfloat16

HEAD_DIM = 128
GRID_W = 64
POOL_WINDOWS = (2, 4, 8, 16)
POOL_GROUP_DIM = 256
N_MOD = 9
EPS = 1e-6
LOG2_E = 1.4426950408889634
PARTS = PART_Q, PART_F_FWD, PART_F_BWD, PART_V, PART_GATE, PART_POOL = tuple(range(6))
ACT_Q, ACT_V, ACT_GATE, ACT_POOL = range(4)

LANES = 128
V7X_VMEM_LIMIT_BYTES = 56 * 1024 * 1024
TOKEN_ROWS = 1024
FFN_HIDDEN_COLS = 512
OUTPROJ_ROWS = 512
EDGE_SUB_ROWS = 256
MOD_ROWS = 8
MOD_COLS = 1024
HGRN_CHUNK = 256
HGRN_HEADS_PER_STEP = 8
POOL_TILE_ROWS = 16
POOL_HALO_ROWS = 8
COL_BLOCK = 256


def _params(*semantics):
    return pltpu.CompilerParams(dimension_semantics=semantics,
                                vmem_limit_bytes=V7X_VMEM_LIMIT_BYTES)


def _sigmoid(x):
    return 1.0 / (1.0 + jnp.exp(-x))


def _neg_abs(d):
    bits = lax.bitcast_convert_type(d, jnp.uint32) | jnp.uint32(0x80000000)
    return lax.bitcast_convert_type(bits, F32)


def _rms(x):
    return x * lax.rsqrt(jnp.mean(x * x, axis=-1, keepdims=True) + EPS)


def _split_bf16(v):
    hi = v.astype(BF16)
    lo = (v - hi.astype(F32)).astype(BF16)
    return hi, lo


def _mod_kernel(c_ref, w_ref, b_ref, o_ref):
    c = c_ref[...]
    s = (c * _sigmoid(c)).astype(BF16)
    o_ref[0] = jnp.dot(s, w_ref[0].astype(BF16), preferred_element_type=F32) + b_ref[0]


def _mod_call(cvec, w_ada, b_ada):
    depth, d, n = w_ada.shape
    tn = MOD_COLS
    return pl.pallas_call(
        _mod_kernel,
        out_shape=jax.ShapeDtypeStruct((depth, MOD_ROWS, n), F32),
        grid=(depth, n // tn),
        in_specs=[
            pl.BlockSpec((MOD_ROWS, d), lambda l, i: (0, 0)),
            pl.BlockSpec((1, d, tn), lambda l, i: (l, 0, i)),
            pl.BlockSpec((1, 1, tn), lambda l, i: (l, 0, i)),
        ],
        out_specs=pl.BlockSpec((1, MOD_ROWS, tn), lambda l, i: (l, 0, i)),
        compiler_params=_params("parallel", "parallel"),
        name="adaln_mod",
    )(cvec, w_ada, b_ada.reshape(depth, 1, n))


def _lane_blocks(d):
    return [slice(s, s + LANES) for s in range(0, d, LANES)]


def _row_tiles(n_rows):
    sub = min(EDGE_SUB_ROWS, n_rows)
    return [slice(r, r + sub) for r in range(0, n_rows, sub)]


def _one_behind(items, produce, consume):
    pending = None
    for item in items:
        made = produce(item)
        if pending is not None:
            consume(*pending)
        pending = (item, made)
    consume(*pending)


def _row_scale(src_ref, rows):
    d = src_ref.shape[2]
    acc = None
    for cols in _lane_blocks(d):
        xb = src_ref[0, rows, cols]
        acc = xb * xb if acc is None else acc + xb * xb
    ss = jnp.sum(acc, axis=-1, keepdims=True)
    return jnp.broadcast_to(lax.rsqrt(ss * (1.0 / d) + EPS), acc.shape)


def _modulated_norm(x_ref, rows, gain, shift, scale, h_ref):
    rs = _row_scale(x_ref, rows)
    for cols in _lane_blocks(x_ref.shape[2]):
        mul = gain[:, cols] * (1.0 + scale[:, cols])
        h_ref[rows, cols] = (x_ref[0, rows, cols] * rs * mul + shift[:, cols]).astype(BF16)


def _ffn_kernel(x_ref, mod_ref, g_ref, wg_ref, wu_ref, wo_ref, o_ref, h_ref, *, j):
    f = pl.program_id(2)
    last = pl.num_programs(2) - 1
    tm, d = h_ref.shape

    def swiglu(rows):
        h = h_ref[rows, :]
        gate = jnp.dot(h, wg_ref[...], preferred_element_type=F32)
        up = jnp.dot(h, wu_ref[...], preferred_element_type=F32)
        a = (gate * _sigmoid(gate) * up).astype(BF16)
        return jnp.dot(a, wo_ref[...], preferred_element_type=F32)

    @pl.when(f == 0)
    def _():
        gain = g_ref[pl.ds(2 * j, 1), :]
        shift = mod_ref[0, pl.ds(3 * j, 1), :]
        scale = mod_ref[0, pl.ds(3 * j + 1, 1), :]
        for rows in _row_tiles(tm):
            _modulated_norm(x_ref, rows, gain, shift, scale, h_ref)
            o_ref[0, rows, :] = swiglu(rows)

    @pl.when((f > 0) & (f < last))
    def _():
        o_ref[0] += swiglu(slice(None))

    @pl.when(f == last)
    def _():
        gain = g_ref[pl.ds(2 * j + 1, 1), :]
        half_gate = 0.5 * mod_ref[0, pl.ds(3 * j + 2, 1), :]
        for rows in _row_tiles(tm):
            o_ref[0, rows, :] += swiglu(rows)
            rs = _row_scale(o_ref, rows)
            for cols in _lane_blocks(d):
                mul = half_gate[:, cols] * gain[:, cols]
                o_ref[0, rows, cols] = x_ref[0, rows, cols] + o_ref[0, rows, cols] * rs * mul


def _ffn_call(x, mod, gains, w_in, w_out, layer, which):
    b, l, d = x.shape
    d_ff = w_out.shape[2]
    tm = min(TOKEN_ROWS, l)
    tf = FFN_HIDDEN_COLS
    nf = d_ff // tf
    assert nf >= 2, "the kernel's first and last grid steps are distinct code paths"
    return pl.pallas_call(
        functools.partial(_ffn_kernel, j=2 * which),
        out_shape=jax.ShapeDtypeStruct((b, l, d), F32),
        grid=(b, l // tm, nf),
        in_specs=[
            pl.BlockSpec((1, tm, d), lambda bi, i, f: (bi, i, 0)),
            pl.BlockSpec((1, N_MOD, d), lambda bi, i, f: (bi, 0, 0)),
            pl.BlockSpec(gains.shape, lambda bi, i, f: (0, 0)),
            pl.BlockSpec((None, None, d, tf), lambda bi, i, f: (layer, which, 0, f)),
            pl.BlockSpec((None, None, d, tf), lambda bi, i, f: (layer, which, 0, f + nf)),
            pl.BlockSpec((None, None, tf, d), lambda bi, i, f: (layer, which, f, 0)),
        ],
        out_specs=pl.BlockSpec((1, tm, d), lambda bi, i, f: (bi, i, 0)),
        scratch_shapes=[pltpu.VMEM((tm, d), BF16)],
        compiler_params=_params("parallel", "parallel", "arbitrary"),
        name="ffn_sublayer",
    )(x, mod, gains, w_in, w_in, w_out)


def _log2_forget(z, lb):
    z2 = z * LOG2_E
    e = jnp.exp2(_neg_abs(z2))
    pos = z > 0
    num = jnp.where(pos, 1.0, lb) + jnp.where(pos, lb * e, e)
    log2_num = jnp.where((lb == 0.0) & jnp.logical_not(pos), z2, jnp.log2(num))
    return log2_num - jnp.log2(1.0 + e)


def _inproj_kernel(x_ref, mod_ref, g_ref, w_ref, lb_ref, z_ref, zg_ref, h_ref):
    part = pl.program_id(2)
    row_tiles = _row_tiles(h_ref.shape[0])

    def project(rows):
        return jnp.dot(h_ref[rows, :], w_ref[...], preferred_element_type=F32)

    def store_silu(rows, a):
        z_ref[0, rows, :] = (a * _sigmoid(a)).astype(BF16)

    @pl.when(part == PART_Q)
    def _():
        gain, shift, scale = g_ref[pl.ds(2, 1), :], mod_ref[0, pl.ds(3, 1), :], mod_ref[0, pl.ds(4, 1), :]

        def norm_project(rows):
            _modulated_norm(x_ref, rows, gain, shift, scale, h_ref)
            return project(rows)

        _one_behind(row_tiles, norm_project, store_silu)

    @pl.when(part == PART_GATE)
    def _():
        _one_behind(row_tiles, project, store_silu)

    for direction, forget_part in enumerate((PART_F_FWD, PART_F_BWD)):
        @pl.when(part == forget_part)
        def _(direction=direction):
            def store_log2_forget(rows, a):
                zg_ref[0, rows, :] = _log2_forget(a, lb_ref[direction:direction + 1, :])

            _one_behind(row_tiles, project, store_log2_forget)

    @pl.when((part == PART_V) | (part == PART_POOL))
    def _():
        z_ref[0] = project(slice(None)).astype(BF16)


def _inproj_call(x, mod, gains, w, layer, lb):
    b, l, d = x.shape
    n = w.shape[2]
    width = lb.shape[1]
    assert n == len(PARTS) * width
    tm = min(TOKEN_ROWS, l)
    z_block = lambda bi, i, k: (bi, i, jnp.maximum(k - PART_F_BWD, 0))
    zg_block = lambda bi, i, k: (bi, i, jnp.clip(k - PART_F_FWD, 0, 1))
    return pl.pallas_call(
        _inproj_kernel,
        out_shape=(jax.ShapeDtypeStruct((b, l, 4 * width), BF16),
                   jax.ShapeDtypeStruct((b, l, 2 * width), F32)),
        grid=(b, l // tm, len(PARTS)),
        in_specs=[
            pl.BlockSpec((1, tm, d), lambda bi, i, k: (bi, i, 0)),
            pl.BlockSpec((1, N_MOD, d), lambda bi, i, k: (bi, 0, 0)),
            pl.BlockSpec(gains.shape, lambda bi, i, k: (0, 0)),
            pl.BlockSpec((None, d, width), lambda bi, i, k: (layer, 0, k)),
            pl.BlockSpec(lb.shape, lambda bi, i, k: (0, 0)),
        ],
        out_specs=(pl.BlockSpec((1, tm, width), z_block), pl.BlockSpec((1, tm, width), zg_block)),
        scratch_shapes=[pltpu.VMEM((tm, d), BF16)],
        compiler_params=_params("parallel", "parallel", "arbitrary"),
        name="mixer_in_proj",
    )(x, mod, gains, w, lb)


def _mid_rows(b_ref, slot, c, half):
    if half >= 4:
        blk = 2 * half
        return jnp.concatenate(
            [jnp.broadcast_to(b_ref[slot, pl.ds(s + half - 1, 1), :], (blk, HEAD_DIM))
             for s in range(0, c, blk)], axis=0)
    sub = lax.broadcasted_iota(jnp.int32, (8, HEAD_DIM), 0)
    vregs = []
    for s in range(0, c, 8):
        lo = jnp.broadcast_to(b_ref[slot, pl.ds(s + 1, 1), :], (8, HEAD_DIM))
        hi = jnp.broadcast_to(b_ref[slot, pl.ds(s + 5, 1), :], (8, HEAD_DIM))
        vregs.append(jnp.where(sub < 4, lo, hi))
    return jnp.concatenate(vregs, axis=0)


def _pick_halves(even_src, odd_src, half, odd_mask):
    c = even_src.shape[0]
    if half < 8:
        return jnp.where(odd_mask, odd_src, even_src)
    shp = (c // (2 * half), 2, half, HEAD_DIM)
    both = jnp.stack([even_src.reshape(shp)[:, 0], odd_src.reshape(shp)[:, 1]], axis=1)
    return both.reshape(c, HEAD_DIM)


def _hgrn_direction(q, v, g, b_ref, slot, code_ref, st, reverse):
    c = q.shape[0]
    hc = c // 2
    k = 1.0 - jnp.exp2(g)
    b = b_ref[slot]
    beta = b - g if reverse else b
    row = lax.broadcasted_iota(jnp.int32, (c, HEAD_DIM), 0)
    def level_products(half):
        odd = (row & half) != 0 if half < 8 else None
        if half == 1:
            u = jnp.where(odd, 0.0, g) if reverse else jnp.where(odd, g, 0.0)
        else:
            u = _neg_abs(beta - _mid_rows(b_ref, slot, c, half))
        src = _pick_halves(q, k, half, odd) if reverse else _pick_halves(k, q, half, odd)
        r = (src * jnp.exp2(u)).astype(BF16)
        return lax.dot_general(r, r, (((1,), (1,)), ((), ())), preferred_element_type=F32)

    first = jnp.zeros((hc, hc), F32)
    second = jnp.zeros((hc, hc), F32)
    half = 1
    while half < hc:
        a = level_products(half)
        mask = code_ref[int(reverse)] >= half
        first = jnp.where(mask, a[:hc, :hc], first)
        second = jnp.where(mask, a[hc:, hc:], second)
        half *= 2
    a = level_products(hc)

    tot = b_ref[slot, pl.ds(c - 1, 1), :]
    q_in, k_out = (tot - beta, beta) if reverse else (b, tot - b)
    q_state = (q * jnp.exp2(q_in)).astype(BF16)
    k_state = (k * jnp.exp2(k_out)).astype(BF16)
    vb = v.astype(BF16)
    if reverse:
        top = jnp.concatenate([first, a[:hc, hc:]], axis=1).astype(BF16)
        o = jnp.concatenate([jnp.dot(top, vb, preferred_element_type=F32),
                             jnp.dot(second.astype(BF16), vb[hc:], preferred_element_type=F32)], axis=0)
    else:
        bottom = jnp.concatenate([a[hc:, :hc], second], axis=1).astype(BF16)
        o = jnp.concatenate([jnp.dot(first.astype(BF16), vb[:hc], preferred_element_type=F32),
                             jnp.dot(bottom, vb, preferred_element_type=F32)], axis=0)
    o += lax.dot_general(q_state, st.astype(BF16), (((1,), (1,)), ((), ())),
                         preferred_element_type=F32)
    o += jnp.sum(q * k, axis=-1, keepdims=True) * v
    st_new = st * jnp.exp2(tot) + lax.dot_general(
        vb, k_state, (((0,), (0,)), ((), ())), preferred_element_type=F32)
    return o, st_new


def _hgrn_kernel(qf_ref, gf_ref, vf_ref, qb_ref, gb_ref, vb_ref,
                 s0f_ref, s0b_ref, of_ref, ob_ref, sf_ref, sb_ref, st_ref, b_ref, code_ref, tri_ref):
    j = pl.program_id(2)
    c = qf_ref.shape[1]
    hp = qf_ref.shape[2] // HEAD_DIM
    lanes = [slice(h * HEAD_DIM, (h + 1) * HEAD_DIM) for h in range(hp)]

    @pl.when(j == 0)
    def _():
        st_ref[0:hp] = s0f_ref[0]
        st_ref[hp:2 * hp] = s0b_ref[0]
        ti = lax.broadcasted_iota(jnp.int32, (c, c), 0)
        si = lax.broadcasted_iota(jnp.int32, (c, c), 1)
        tri_ref[...] = jnp.where(ti >= si, 1.0, 0.0).astype(BF16)
        ti = lax.broadcasted_iota(jnp.int32, (c // 2, c // 2), 0)
        si = lax.broadcasted_iota(jnp.int32, (c // 2, c // 2), 1)
        level = ti ^ si
        code_ref[0] = jnp.where(ti > si, level, 0)
        code_ref[1] = jnp.where(ti < si, level, 0)

    hi, lo = _split_bf16(jnp.concatenate([gf_ref[0], gb_ref[0]], axis=1))
    b2 = (jnp.dot(tri_ref[...], hi, preferred_element_type=F32)
          + jnp.dot(tri_ref[...], lo, preferred_element_type=F32))
    for slot in range(2 * hp):
        b_ref[slot] = b2[:, slot * HEAD_DIM:(slot + 1) * HEAD_DIM]

    for h, cols in enumerate(lanes):
        o, st = _hgrn_direction(qf_ref[0, :, cols].astype(F32), vf_ref[0, :, cols].astype(F32),
                                gf_ref[0, :, cols], b_ref, h, code_ref, st_ref[h], False)
        of_ref[0, :, cols] = o
        st_ref[h] = st
        o, st = _hgrn_direction(qb_ref[0, :, cols].astype(F32), vb_ref[0, :, cols].astype(F32),
                                gb_ref[0, :, cols], b_ref, hp + h, code_ref, st_ref[hp + h], True)
        ob_ref[0, :, cols] = o
        st_ref[hp + h] = st

    @pl.when(j == pl.num_programs(2) - 1)
    def _():
        sf_ref[0] = st_ref[0:hp]
        sb_ref[0] = st_ref[hp:2 * hp]


def _hgrn_call(z, zg, s0f, s0b):
    b, l, n = zg.shape
    width = n // 2
    heads = width // HEAD_DIM
    hp = min(HGRN_HEADS_PER_STEP, heads)
    groups = heads // hp
    c = min(HGRN_CHUNK, l)
    n = l // c
    col = lambda part: (lambda bi, h, j: (bi, j, part * groups + h))
    col_rev = lambda part: (lambda bi, h, j: (bi, n - 1 - j, part * groups + h))
    tok = lambda imap: pl.BlockSpec((1, c, hp * HEAD_DIM), imap)
    st_spec = pl.BlockSpec((1, hp, HEAD_DIM, HEAD_DIM), lambda bi, h, j: (bi, h, 0, 0))
    st_shape = jax.ShapeDtypeStruct((b, heads, HEAD_DIM, HEAD_DIM), F32)
    return pl.pallas_call(
        _hgrn_kernel,
        out_shape=(jax.ShapeDtypeStruct((b, l, width), F32),
                   jax.ShapeDtypeStruct((b, l, width), F32), st_shape, st_shape),
        grid=(b, groups, n),
        in_specs=[tok(col(ACT_Q)), tok(col(0)), tok(col(ACT_V)),
                  tok(col_rev(ACT_Q)), tok(col_rev(1)), tok(col_rev(ACT_V)),
                  st_spec, st_spec],
        out_specs=(tok(lambda bi, h, j: (bi, j, h)), tok(lambda bi, h, j: (bi, n - 1 - j, h)),
                   st_spec, st_spec),
        scratch_shapes=[pltpu.VMEM((2 * hp, HEAD_DIM, HEAD_DIM), F32),
                        pltpu.VMEM((2 * hp, c, HEAD_DIM), F32),
                        pltpu.VMEM((2, c // 2, c // 2), jnp.int32),
                        pltpu.VMEM((c, c), BF16)],
        compiler_params=_params("parallel", "parallel", "arbitrary"),
        name="hgrn2_scan",
    )(z, zg, z, z, zg, z, s0f, s0b)


def _pool_project(m, v, g, wp_ref, bp_ref, ps_ref, y_ref):
    cols = slice(g * POOL_GROUP_DIM, (g + 1) * POOL_GROUP_DIM)
    y = jnp.dot((m - v).astype(BF16), wp_ref[g], preferred_element_type=F32)
    y_ref[0, :, cols] = ((y + bp_ref[:, cols]) * ps_ref[:, cols]).astype(y_ref.dtype)


def _pool2d_kernel(prev_ref, cur_ref, next_ref, wp_ref, bp_ref, ps_ref, y_ref, cs_ref, *, rows):
    i = pl.program_id(1)
    tile = cur_ref.shape[1]

    tok = lax.broadcasted_iota(jnp.int32, (tile, LANES), 0)
    grow = tok // GRID_W + i * POOL_TILE_ROWS
    gcol = tok % GRID_W
    eo = lax.broadcasted_iota(jnp.int32, (COL_BLOCK, COL_BLOCK), 0)
    ei = lax.broadcasted_iota(jnp.int32, (COL_BLOCK, COL_BLOCK), 1)
    same_row = (eo // GRID_W) == (ei // GRID_W)
    dcol = ei % GRID_W - eo % GRID_W

    for g, w in enumerate(POOL_WINDOWS):
        cols = slice(g * POOL_GROUP_DIM, (g + 1) * POOL_GROUP_DIM)
        back = w // 2
        band = jnp.where(same_row & (dcol >= -back) & (dcol < w - back), 1.0, 0.0).astype(BF16)
        zero = jnp.zeros((), BF16)
        pieces = (
            jnp.where(i > 0, prev_ref[0, :, cols], zero),
            cur_ref[0, :, cols],
            jnp.where(i < pl.num_programs(1) - 1, next_ref[0, :, cols], zero),
        )
        off = 0
        for piece in pieces:
            for s in range(0, piece.shape[0], COL_BLOCK):
                cs_ref[off + s:off + s + COL_BLOCK, :] = jnp.dot(
                    band, piece[s:s + COL_BLOCK], preferred_element_type=F32)
            off += piece.shape[0]
        start = (POOL_HALO_ROWS - back) * GRID_W
        n_tok = tile + (w - 1) * GRID_W
        acc = cs_ref[start:start + n_tok, :]
        span = 1
        while span < w:
            n_tok -= span * GRID_W
            acc = acc[:n_tok] + acc[span * GRID_W:span * GRID_W + n_tok]
            span *= 2
        cnt_r = jnp.minimum(grow + (w - back), rows) - jnp.maximum(grow - back, 0)
        cnt_c = jnp.minimum(gcol + (w - back), GRID_W) - jnp.maximum(gcol - back, 0)
        inv = 1.0 / (cnt_r * cnt_c).astype(F32)
        m = acc * jnp.concatenate([inv] * (POOL_GROUP_DIM // LANES), axis=1)
        _pool_project(m, cur_ref[0, :, cols].astype(F32), g, wp_ref, bp_ref, ps_ref, y_ref)


def _pool2d_call(z, w_pool, b_pool, pool_scale, rows):
    b, l, n = z.shape
    width = w_pool.shape[0] * POOL_GROUP_DIM
    p_blk = ACT_POOL
    assert rows % POOL_TILE_ROWS == 0 and POOL_TILE_ROWS % POOL_HALO_ROWS == 0
    assert max(POOL_WINDOWS) // 2 <= POOL_HALO_ROWS
    tile = POOL_TILE_ROWS * GRID_W
    halo = POOL_HALO_ROWS * GRID_W
    n_halo = l // halo
    per = tile // halo
    vec = pl.BlockSpec((1, width), lambda bi, i: (0, 0))
    return pl.pallas_call(
        functools.partial(_pool2d_kernel, rows=rows),
        out_shape=jax.ShapeDtypeStruct((b, l, width), BF16),
        grid=(b, l // tile),
        in_specs=[
            pl.BlockSpec((1, halo, width), lambda bi, i: (bi, jnp.maximum(i * per - 1, 0), p_blk)),
            pl.BlockSpec((1, tile, width), lambda bi, i: (bi, i, p_blk)),
            pl.BlockSpec((1, halo, width),
                         lambda bi, i: (bi, jnp.minimum((i + 1) * per, n_halo - 1), p_blk)),
            pl.BlockSpec(w_pool.shape, lambda bi, i: (0, 0, 0)),
            vec, vec,
        ],
        out_specs=pl.BlockSpec((1, tile, width), lambda bi, i: (bi, i, 0)),
        scratch_shapes=[pltpu.VMEM((halo + tile + halo, POOL_GROUP_DIM), F32)],
        compiler_params=_params("parallel", "parallel"),
        name="pool_mixer_2d",
    )(z, z, z, w_pool, b_pool, pool_scale)


def _pool1d_kernel(p_ref, wp_ref, bp_ref, ps_ref, y_ref):
    l = p_ref.shape[1]
    to = lax.broadcasted_iota(jnp.int32, (l, l), 0)
    dt = lax.broadcasted_iota(jnp.int32, (l, l), 1) - to
    pos = lax.broadcasted_iota(jnp.int32, (l, POOL_GROUP_DIM), 0)
    for g, w in enumerate(POOL_WINDOWS):
        cols = slice(g * POOL_GROUP_DIM, (g + 1) * POOL_GROUP_DIM)
        back = w // 2
        band = jnp.where((dt >= -back) & (dt < w - back), 1.0, 0.0).astype(BF16)
        v = p_ref[0, :, cols]
        total = jnp.dot(band, v, preferred_element_type=F32)
        cnt = jnp.minimum(pos + (w - back), l) - jnp.maximum(pos - back, 0)
        _pool_project(total / cnt.astype(F32), v.astype(F32), g, wp_ref, bp_ref, ps_ref, y_ref)


def _pool1d_call(z, w_pool, b_pool, pool_scale):
    b, l, n = z.shape
    width = w_pool.shape[0] * POOL_GROUP_DIM
    p_blk = ACT_POOL
    vec = pl.BlockSpec((1, width), lambda bi: (0, 0))
    return pl.pallas_call(
        _pool1d_kernel,
        out_shape=jax.ShapeDtypeStruct((b, l, width), BF16),
        grid=(b,),
        in_specs=[pl.BlockSpec((1, l, width), lambda bi: (bi, 0, p_blk)),
                  pl.BlockSpec(w_pool.shape, lambda bi: (0, 0, 0)), vec, vec],
        out_specs=pl.BlockSpec((1, l, width), lambda bi: (bi, 0, 0)),
        compiler_params=_params("parallel"),
        name="pool_mixer_1d",
    )(z, w_pool, b_pool, pool_scale)


def _outproj_kernel(of_ref, ob_ref, gate_ref, y_ref, x_ref, mod_ref, g_ref, hg_ref, w_ref, o_ref):
    width = of_ref.shape[2]
    o = of_ref[0] + ob_ref[0]
    heads = []
    for h in range(width // HEAD_DIM):
        heads.append(_rms(o[:, h * HEAD_DIM:(h + 1) * HEAD_DIM]) * hg_ref[...])
    og = (jnp.concatenate(heads, axis=-1) * gate_ref[0].astype(F32)).astype(BF16)
    mix = jnp.dot(jnp.concatenate([og, y_ref[0]], axis=-1), w_ref[...], preferred_element_type=F32)
    r = _rms(mix) * g_ref[pl.ds(3, 1), :]
    o_ref[0] = x_ref[0] + mod_ref[0, pl.ds(5, 1), :] * r


def _outproj_call(o_fw, o_bw, z, y, x, mod, gains, hgrn_gain, w, layer):
    b, l, d = x.shape
    width = o_fw.shape[2]
    g_blk = ACT_GATE
    tm = min(OUTPROJ_ROWS, l)
    tokw = lambda imap: pl.BlockSpec((1, tm, width), imap)
    row = lambda bi, i: (bi, i, 0)
    return pl.pallas_call(
        _outproj_kernel,
        out_shape=jax.ShapeDtypeStruct((b, l, d), F32),
        grid=(b, l // tm),
        in_specs=[
            tokw(row), tokw(row), tokw(lambda bi, i: (bi, i, g_blk)), tokw(row),
            pl.BlockSpec((1, tm, d), row),
            pl.BlockSpec((1, N_MOD, d), lambda bi, i: (bi, 0, 0)),
            pl.BlockSpec(gains.shape, lambda bi, i: (0, 0)),
            pl.BlockSpec((1, HEAD_DIM), lambda bi, i: (0, 0)),
            pl.BlockSpec((None,) + w.shape[1:], lambda bi, i: (layer, 0, 0)),
        ],
        out_specs=pl.BlockSpec((1, tm, d), row),
        compiler_params=_params("parallel", "parallel"),
        name="mixer_out_proj",
    )(o_fw, o_bw, z, y, x, mod, gains, hgrn_gain, w)


def kernel(x, c, ctx, c_ctx, w_ada, b_ada, norm_gain, ffn_in, ffn_out, w_in, hgrn_lb, hgrn_gain,
           w_pool, b_pool, pool_scale, w_out):
    bsz, seq, d = x.shape
    depth = w_ada.shape[0]
    rows = seq // GRID_W
    width = hgrn_lb.shape[-1]
    heads = width // HEAD_DIM

    lbp = jax.nn.softmax(hgrn_lb.astype(F32), axis=0)
    lower = jnp.cumsum(lbp, axis=0) - lbp[0]

    cvec = jnp.zeros((MOD_ROWS, d), F32).at[:bsz].set(c).at[bsz].set(c_ctx)
    mod_all = _mod_call(cvec, w_ada, b_ada)

    ffn_in_b, ffn_out_b = ffn_in.astype(BF16), ffn_out.astype(BF16)
    w_in_b, w_out_b, w_pool_b = w_in.astype(BF16), w_out.astype(BF16), w_pool.astype(BF16)

    s0 = jnp.zeros((bsz, heads, HEAD_DIM, HEAD_DIM), F32)
    ctx_len = ctx.shape[1]
    ctx = ctx.reshape(1, bsz * ctx_len, d)
    per_sample = lambda a: a.reshape(bsz, ctx_len, a.shape[-1])
    flat = lambda a: a.reshape(1, bsz * ctx_len, a.shape[-1])
    for l in range(depth):
        last = l == depth - 1
        gains = norm_gain[l]
        mod_x = mod_all[l, :bsz].reshape(bsz, N_MOD, d)
        mod_c = mod_all[l, bsz].reshape(1, N_MOD, d)
        hg = hgrn_gain[l].reshape(1, HEAD_DIM)
        bp = b_pool[l].reshape(1, -1)
        ps = pool_scale[l].reshape(1, -1)

        x = _ffn_call(x, mod_x, gains, ffn_in_b, ffn_out_b, l, 0)
        ctx = _ffn_call(ctx, mod_c, gains, ffn_in_b, ffn_out_b, l, 0)

        z_x, zg_x = _inproj_call(x, mod_x, gains, w_in_b, l, lower[l])
        z_c, zg_c = _inproj_call(ctx, mod_c, gains, w_in_b, l, lower[l])
        oc_fw, oc_bw, s_fw, s_bw = _hgrn_call(per_sample(z_c), per_sample(zg_c), s0, s0)
        ox_fw, ox_bw, _, _ = _hgrn_call(z_x, zg_x, s_fw, s_bw)
        y_x = _pool2d_call(z_x, w_pool_b[l], bp, ps, rows)
        x = _outproj_call(ox_fw, ox_bw, z_x, y_x, x, mod_x, gains, hg, w_out_b, l)
        if not last:
            y_c = _pool1d_call(per_sample(z_c), w_pool_b[l], bp, ps)
            ctx = _outproj_call(flat(oc_fw), flat(oc_bw), z_c, flat(y_c), ctx, mod_c, gains, hg, w_out_b, l)
            ctx = _ffn_call(ctx, mod_c, gains, ffn_in_b, ffn_out_b, l, 1)

        x = _ffn_call(x, mod_x, gains, ffn_in_b, ffn_out_b, l, 1)
    return x
```

```python
import functools

import jax
import jax.numpy as jnp
from jax import lax
from jax.experimental import pallas as pl
from jax.experimental.pallas import tpu as pltpu

F32 = jnp.float32
BF16 = jnp.bfloat16

HEAD_DIM = 128
GRID_W = 64
POOL_WINDOWS = (2, 4, 8, 16)
POOL_GROUP_DIM = 256
N_MOD = 9
EPS = 1e-6
LOG2_E = 1.4426950408889634
PARTS = PART_Q, PART_F_FWD, PART_F_BWD, PART_V, PART_GATE, PART_POOL = tuple(range(6))
ACT_Q, ACT_V, ACT_GATE, ACT_POOL = range(4)

LANES = 128
V7X_VMEM_LIMIT_BYTES = 56 * 1024 * 1024
TOKEN_ROWS = 1024
FFN_HIDDEN_COLS = 512
OUTPROJ_ROWS = 512
EDGE_SUB_ROWS = 256
MOD_ROWS = 8
MOD_COLS = 1024
HGRN_CHUNK = 256
HGRN_HEADS_PER_STEP = 8
POOL_TILE_ROWS = 16
POOL_HALO_ROWS = 8
COL_BLOCK = 256


def _params(*semantics):
    return pltpu.CompilerParams(dimension_semantics=semantics,
                                vmem_limit_bytes=V7X_VMEM_LIMIT_BYTES)


def _sigmoid(x):
    return 1.0 / (1.0 + jnp.exp(-x))


def _neg_abs(d):
    bits = lax.bitcast_convert_type(d, jnp.uint32) | jnp.uint32(0x80000000)
    return lax.bitcast_convert_type(bits, F32)


def _rms(x):
    return x * lax.rsqrt(jnp.mean(x * x, axis=-1, keepdims=True) + EPS)


def _split_bf16(v):
    hi = v.astype(BF16)
    lo = (v - hi.astype(F32)).astype(BF16)
    return hi, lo


def _mod_kernel(c_ref, w_ref, b_ref, o_ref):
    c = c_ref[...]
    s = (c * _sigmoid(c)).astype(BF16)
    o_ref[0] = jnp.dot(s, w_ref[0].astype(BF16), preferred_element_type=F32) + b_ref[0]


def _mod_call(cvec, w_ada, b_ada):
    depth, d, n = w_ada.shape
    tn = MOD_COLS
    return pl.pallas_call(
        _mod_kernel,
        out_shape=jax.ShapeDtypeStruct((depth, MOD_ROWS, n), F32),
        grid=(depth, n // tn),
        in_specs=[
            pl.BlockSpec((MOD_ROWS, d), lambda l, i: (0, 0)),
            pl.BlockSpec((1, d, tn), lambda l, i: (l, 0, i)),
            pl.BlockSpec((1, 1, tn), lambda l, i: (l, 0, i)),
        ],
        out_specs=pl.BlockSpec((1, MOD_ROWS, tn), lambda l, i: (l, 0, i)),
        compiler_params=_params("parallel", "parallel"),
        name="adaln_mod",
    )(cvec, w_ada, b_ada.reshape(depth, 1, n))


def _lane_blocks(d):
    return [slice(s, s + LANES) for s in range(0, d, LANES)]


def _row_tiles(n_rows):
    sub = min(EDGE_SUB_ROWS, n_rows)
    return [slice(r, r + sub) for r in range(0, n_rows, sub)]


def _one_behind(items, produce, consume):
    pending = None
    for item in items:
        made = produce(item)
        if pending is not None:
            consume(*pending)
        pending = (item, made)
    consume(*pending)


def _row_scale(src_ref, rows):
    d = src_ref.shape[2]
    acc = None
    for cols in _lane_blocks(d):
        xb = src_ref[0, rows, cols]
        acc = xb * xb if acc is None else acc + xb * xb
    ss = jnp.sum(acc, axis=-1, keepdims=True)
    return jnp.broadcast_to(lax.rsqrt(ss * (1.0 / d) + EPS), acc.shape)


def _modulated_norm(x_ref, rows, gain, shift, scale, h_ref):
    rs = _row_scale(x_ref, rows)
    for cols in _lane_blocks(x_ref.shape[2]):
        mul = gain[:, cols] * (1.0 + scale[:, cols])
        h_ref[rows, cols] = (x_ref[0, rows, cols] * rs * mul + shift[:, cols]).astype(BF16)


def _ffn_kernel(x_ref, mod_ref, g_ref, wg_ref, wu_ref, wo_ref, o_ref, h_ref, *, j):
    f = pl.program_id(2)
    last = pl.num_programs(2) - 1
    tm, d = h_ref.shape

    def swiglu(rows):
        h = h_ref[rows, :]
        gate = jnp.dot(h, wg_ref[...], preferred_element_type=F32)
        up = jnp.dot(h, wu_ref[...], preferred_element_type=F32)
        a = (gate * _sigmoid(gate) * up).astype(BF16)
        return jnp.dot(a, wo_ref[...], preferred_element_type=F32)

    @pl.when(f == 0)
    def _():
        gain = g_ref[pl.ds(2 * j, 1), :]
        shift = mod_ref[0, pl.ds(3 * j, 1), :]
        scale = mod_ref[0, pl.ds(3 * j + 1, 1), :]
        for rows in _row_tiles(tm):
            _modulated_norm(x_ref, rows, gain, shift, scale, h_ref)
            o_ref[0, rows, :] = swiglu(rows)

    @pl.when((f > 0) & (f < last))
    def _():
        o_ref[0] += swiglu(slice(None))

    @pl.when(f == last)
    def _():
        gain = g_ref[pl.ds(2 * j + 1, 1), :]
        half_gate = 0.5 * mod_ref[0, pl.ds(3 * j + 2, 1), :]
        for rows in _row_tiles(tm):
            o_ref[0, rows, :] += swiglu(rows)
            rs = _row_scale(o_ref, rows)
            for cols in _lane_blocks(d):
                mul = half_gate[:, cols] * gain[:, cols]
                o_ref[0, rows, cols] = x_ref[0, rows, cols] + o_ref[0, rows, cols] * rs * mul


def _ffn_call(x, mod, gains, w_in, w_out, layer, which):
    b, l, d = x.shape
    d_ff = w_out.shape[2]
    tm = min(TOKEN_ROWS, l)
    tf = FFN_HIDDEN_COLS
    nf = d_ff // tf
    assert nf >= 2, "the kernel's first and last grid steps are distinct code paths"
    return pl.pallas_call(
        functools.partial(_ffn_kernel, j=2 * which),
        out_shape=jax.ShapeDtypeStruct((b, l, d), F32),
        grid=(b, l // tm, nf),
        in_specs=[
            pl.BlockSpec((1, tm, d), lambda bi, i, f: (bi, i, 0)),
            pl.BlockSpec((1, N_MOD, d), lambda bi, i, f: (bi, 0, 0)),
            pl.BlockSpec(gains.shape, lambda bi, i, f: (0, 0)),
            pl.BlockSpec((None, None, d, tf), lambda bi, i, f: (layer, which, 0, f)),
            pl.BlockSpec((None, None, d, tf), lambda bi, i, f: (layer, which, 0, f + nf)),
            pl.BlockSpec((None, None, tf, d), lambda bi, i, f: (layer, which, f, 0)),
        ],
        out_specs=pl.BlockSpec((1, tm, d), lambda bi, i, f: (bi, i, 0)),
        scratch_shapes=[pltpu.VMEM((tm, d), BF16)],
        compiler_params=_params("parallel", "parallel", "arbitrary"),
        name="ffn_sublayer",
    )(x, mod, gains, w_in, w_in, w_out)


def _log2_forget(z, lb):
    z2 = z * LOG2_E
    e = jnp.exp2(_neg_abs(z2))
    pos = z > 0
    num = jnp.where(pos, 1.0, lb) + jnp.where(pos, lb * e, e)
    log2_num = jnp.where((lb == 0.0) & jnp.logical_not(pos), z2, jnp.log2(num))
    return log2_num - jnp.log2(1.0 + e)


def _inproj_kernel(x_ref, mod_ref, g_ref, w_ref, lb_ref, z_ref, zg_ref, h_ref):
    part = pl.program_id(2)
    row_tiles = _row_tiles(h_ref.shape[0])

    def project(rows):
        return jnp.dot(h_ref[rows, :], w_ref[...], preferred_element_type=F32)

    def store_silu(rows, a):
        z_ref[0, rows, :] = (a * _sigmoid(a)).astype(BF16)

    @pl.when(part == PART_Q)
    def _():
        gain, shift, scale = g_ref[pl.ds(2, 1), :], mod_ref[0, pl.ds(3, 1), :], mod_ref[0, pl.ds(4, 1), :]

        def norm_project(rows):
            _modulated_norm(x_ref, rows, gain, shift, scale, h_ref)
            return project(rows)

        _one_behind(row_tiles, norm_project, store_silu)

    @pl.when(part == PART_GATE)
    def _():
        _one_behind(row_tiles, project, store_silu)

    for direction, forget_part in enumerate((PART_F_FWD, PART_F_BWD)):
        @pl.when(part == forget_part)
        def _(direction=direction):
            def store_log2_forget(rows, a):
                zg_ref[0, rows, :] = _log2_forget(a, lb_ref[direction:direction + 1, :])

            _one_behind(row_tiles, project, store_log2_forget)

    @pl.when((part == PART_V) | (part == PART_POOL))
    def _():
        z_ref[0] = project(slice(None)).astype(BF16)


def _inproj_call(x, mod, gains, w, layer, lb):
    b, l, d = x.shape
    n = w.shape[2]
    width = lb.shape[1]
    assert n == len(PARTS) * width
    tm = min(TOKEN_ROWS, l)
    z_block = lambda bi, i, k: (bi, i, jnp.maximum(k - PART_F_BWD, 0))
    zg_block = lambda bi, i, k: (bi, i, jnp.clip(k - PART_F_FWD, 0, 1))
    return pl.pallas_call(
        _inproj_kernel,
        out_shape=(jax.ShapeDtypeStruct((b, l, 4 * width), BF16),
                   jax.ShapeDtypeStruct((b, l, 2 * width), F32)),
        grid=(b, l // tm, len(PARTS)),
        in_specs=[
            pl.BlockSpec((1, tm, d), lambda bi, i, k: (bi, i, 0)),
            pl.BlockSpec((1, N_MOD, d), lambda bi, i, k: (bi, 0, 0)),
            pl.BlockSpec(gains.shape, lambda bi, i, k: (0, 0)),
            pl.BlockSpec((None, d, width), lambda bi, i, k: (layer, 0, k)),
            pl.BlockSpec(lb.shape, lambda bi, i, k: (0, 0)),
        ],
        out_specs=(pl.BlockSpec((1, tm, width), z_block), pl.BlockSpec((1, tm, width), zg_block)),
        scratch_shapes=[pltpu.VMEM((tm, d), BF16)],
        compiler_params=_params("parallel", "parallel", "arbitrary"),
        name="mixer_in_proj",
    )(x, mod, gains, w, lb)


def _mid_rows(b_ref, slot, c, half):
    if half >= 4:
        blk = 2 * half
        return jnp.concatenate(
            [jnp.broadcast_to(b_ref[slot, pl.ds(s + half - 1, 1), :], (blk, HEAD_DIM))
             for s in range(0, c, blk)], axis=0)
    sub = lax.broadcasted_iota(jnp.int32, (8, HEAD_DIM), 0)
    vregs = []
    for s in range(0, c, 8):
        lo = jnp.broadcast_to(b_ref[slot, pl.ds(s + 1, 1), :], (8, HEAD_DIM))
        hi = jnp.broadcast_to(b_ref[slot, pl.ds(s + 5, 1), :], (8, HEAD_DIM))
        vregs.append(jnp.where(sub < 4, lo, hi))
    return jnp.concatenate(vregs, axis=0)


def _pick_halves(even_src, odd_src, half, odd_mask):
    c = even_src.shape[0]
    if half < 8:
        return jnp.where(odd_mask, odd_src, even_src)
    shp = (c // (2 * half), 2, half, HEAD_DIM)
    both = jnp.stack([even_src.reshape(shp)[:, 0], odd_src.reshape(shp)[:, 1]], axis=1)
    return both.reshape(c, HEAD_DIM)


def _hgrn_direction(q, v, g, b_ref, slot, code_ref, st, reverse):
    c = q.shape[0]
    hc = c // 2
    k = 1.0 - jnp.exp2(g)
    b = b_ref[slot]
    beta = b - g if reverse else b
    row = lax.broadcasted_iota(jnp.int32, (c, HEAD_DIM), 0)
    def level_products(half):
        odd = (row & half) != 0 if half < 8 else None
        if half == 1:
            u = jnp.where(odd, 0.0, g) if reverse else jnp.where(odd, g, 0.0)
        else:
            u = _neg_abs(beta - _mid_rows(b_ref, slot, c, half))
        src = _pick_halves(q, k, half, odd) if reverse else _pick_halves(k, q, half, odd)
        r = (src * jnp.exp2(u)).astype(BF16)
        return lax.dot_general(r, r, (((1,), (1,)), ((), ())), preferred_element_type=F32)

    first = jnp.zeros((hc, hc), F32)
    second = jnp.zeros((hc, hc), F32)
    half = 1
    while half < hc:
        a = level_products(half)
        mask = code_ref[int(reverse)] >= half
        first = jnp.where(mask, a[:hc, :hc], first)
        second = jnp.where(mask, a[hc:, hc:], second)
        half *= 2
    a = level_products(hc)

    tot = b_ref[slot, pl.ds(c - 1, 1), :]
    q_in, k_out = (tot - beta, beta) if reverse else (b, tot - b)
    q_state = (q * jnp.exp2(q_in)).astype(BF16)
    k_state = (k * jnp.exp2(k_out)).astype(BF16)
    vb = v.astype(BF16)
    if reverse:
        top = jnp.concatenate([first, a[:hc, hc:]], axis=1).astype(BF16)
        o = jnp.concatenate([jnp.dot(top, vb, preferred_element_type=F32),
                             jnp.dot(second.astype(BF16), vb[hc:], preferred_element_type=F32)], axis=0)
    else:
        bottom = jnp.concatenate([a[hc:, :hc], second], axis=1).astype(BF16)
        o = jnp.concatenate([jnp.dot(first.astype(BF16), vb[:hc], preferred_element_type=F32),
                             jnp.dot(bottom, vb, preferred_element_type=F32)], axis=0)
    o += lax.dot_general(q_state, st.astype(BF16), (((1,), (1,)), ((), ())),
                         preferred_element_type=F32)
    o += jnp.sum(q * k, axis=-1, keepdims=True) * v
    st_new = st * jnp.exp2(tot) + lax.dot_general(
        vb, k_state, (((0,), (0,)), ((), ())), preferred_element_type=F32)
    return o, st_new


def _hgrn_kernel(qf_ref, gf_ref, vf_ref, qb_ref, gb_ref, vb_ref,
                 s0f_ref, s0b_ref, of_ref, ob_ref, sf_ref, sb_ref, st_ref, b_ref, code_ref, tri_ref):
    j = pl.program_id(2)
    c = qf_ref.shape[1]
    hp = qf_ref.shape[2] // HEAD_DIM
    lanes = [slice(h * HEAD_DIM, (h + 1) * HEAD_DIM) for h in range(hp)]

    @pl.when(j == 0)
    def _():
        st_ref[0:hp] = s0f_ref[0]
        st_ref[hp:2 * hp] = s0b_ref[0]
        ti = lax.broadcasted_iota(jnp.int32, (c, c), 0)
        si = lax.broadcasted_iota(jnp.int32, (c, c), 1)
        tri_ref[...] = jnp.where(ti >= si, 1.0, 0.0).astype(BF16)
        ti = lax.broadcasted_iota(jnp.int32, (c // 2, c // 2), 0)
        si = lax.broadcasted_iota(jnp.int32, (c // 2, c // 2), 1)
        level = ti ^ si
        code_ref[0] = jnp.where(ti > si, level, 0)
        code_ref[1] = jnp.where(ti < si, level, 0)

    hi, lo = _split_bf16(jnp.concatenate([gf_ref[0], gb_ref[0]], axis=1))
    b2 = (jnp.dot(tri_ref[...], hi, preferred_element_type=F32)
          + jnp.dot(tri_ref[...], lo, preferred_element_type=F32))
    for slot in range(2 * hp):
        b_ref[slot] = b2[:, slot * HEAD_DIM:(slot + 1) * HEAD_DIM]

    for h, cols in enumerate(lanes):
        o, st = _hgrn_direction(qf_ref[0, :, cols].astype(F32), vf_ref[0, :, cols].astype(F32),
                                gf_ref[0, :, cols], b_ref, h, code_ref, st_ref[h], False)
        of_ref[0, :, cols] = o.astype(of_ref.dtype)
        st_ref[h] = st
        o, st = _hgrn_direction(qb_ref[0, :, cols].astype(F32), vb_ref[0, :, cols].astype(F32),
                                gb_ref[0, :, cols], b_ref, hp + h, code_ref, st_ref[hp + h], True)
        ob_ref[0, :, cols] = o.astype(ob_ref.dtype)
        st_ref[hp + h] = st

    @pl.when(j == pl.num_programs(2) - 1)
    def _():
        sf_ref[0] = st_ref[0:hp]
        sb_ref[0] = st_ref[hp:2 * hp]


def _hgrn_call(z, zg, s0f, s0b):
    b, l, n = zg.shape
    width = n // 2
    heads = width // HEAD_DIM
    hp = min(HGRN_HEADS_PER_STEP, heads)
    groups = heads // hp
    c = min(HGRN_CHUNK, l)
    n = l // c
    col = lambda part: (lambda bi, h, j: (bi, j, part * groups + h))
    col_rev = lambda part: (lambda bi, h, j: (bi, n - 1 - j, part * groups + h))
    tok = lambda imap: pl.BlockSpec((1, c, hp * HEAD_DIM), imap)
    st_spec = pl.BlockSpec((1, hp, HEAD_DIM, HEAD_DIM), lambda bi, h, j: (bi, h, 0, 0))
    st_shape = jax.ShapeDtypeStruct((b, heads, HEAD_DIM, HEAD_DIM), F32)
    return pl.pallas_call(
        _hgrn_kernel,
        out_shape=(jax.ShapeDtypeStruct((b, l, width), BF16),
                   jax.ShapeDtypeStruct((b, l, width), BF16), st_shape, st_shape),
        grid=(b, groups, n),
        in_specs=[tok(col(ACT_Q)), tok(col(0)), tok(col(ACT_V)),
                  tok(col_rev(ACT_Q)), tok(col_rev(1)), tok(col_rev(ACT_V)),
                  st_spec, st_spec],
        out_specs=(tok(lambda bi, h, j: (bi, j, h)), tok(lambda bi, h, j: (bi, n - 1 - j, h)),
                   st_spec, st_spec),
        scratch_shapes=[pltpu.VMEM((2 * hp, HEAD_DIM, HEAD_DIM), F32),
                        pltpu.VMEM((2 * hp, c, HEAD_DIM), F32),
                        pltpu.VMEM((2, c // 2, c // 2), jnp.int32),
                        pltpu.VMEM((c, c), BF16)],
        compiler_params=_params("parallel", "parallel", "arbitrary"),
        name="hgrn2_scan",
    )(z, zg, z, z, zg, z, s0f, s0b)


def _pool_project(m, v, g, wp_ref, bp_ref, ps_ref, y_ref):
    cols = slice(g * POOL_GROUP_DIM, (g + 1) * POOL_GROUP_DIM)
    y = jnp.dot((m - v).astype(BF16), wp_ref[g], preferred_element_type=F32)
    y_ref[0, :, cols] = ((y + bp_ref[:, cols]) * ps_ref[:, cols]).astype(y_ref.dtype)


def _pool2d_kernel(prev_ref, cur_ref, next_ref, wp_ref, bp_ref, ps_ref, y_ref, cs_ref, *, rows):
    i = pl.program_id(1)
    tile = cur_ref.shape[1]

    tok = lax.broadcasted_iota(jnp.int32, (tile, LANES), 0)
    grow = tok // GRID_W + i * POOL_TILE_ROWS
    gcol = tok % GRID_W
    eo = lax.broadcasted_iota(jnp.int32, (COL_BLOCK, COL_BLOCK), 0)
    ei = lax.broadcasted_iota(jnp.int32, (COL_BLOCK, COL_BLOCK), 1)
    same_row = (eo // GRID_W) == (ei // GRID_W)
    dcol = ei % GRID_W - eo % GRID_W

    for g, w in enumerate(POOL_WINDOWS):
        cols = slice(g * POOL_GROUP_DIM, (g + 1) * POOL_GROUP_DIM)
        back = w // 2
        band = jnp.where(same_row & (dcol >= -back) & (dcol < w - back), 1.0, 0.0).astype(BF16)
        zero = jnp.zeros((), BF16)
        pieces = (
            jnp.where(i > 0, prev_ref[0, :, cols], zero),
            cur_ref[0, :, cols],
            jnp.where(i < pl.num_programs(1) - 1, next_ref[0, :, cols], zero),
        )
        off = 0
        for piece in pieces:
            for s in range(0, piece.shape[0], COL_BLOCK):
                cs_ref[off + s:off + s + COL_BLOCK, :] = jnp.dot(
                    band, piece[s:s + COL_BLOCK], preferred_element_type=F32)
            off += piece.shape[0]
        start = (POOL_HALO_ROWS - back) * GRID_W
        n_tok = tile + (w - 1) * GRID_W
        acc = cs_ref[start:start + n_tok, :]
        span = 1
        while span < w:
            n_tok -= span * GRID_W
            acc = acc[:n_tok] + acc[span * GRID_W:span * GRID_W + n_tok]
            span *= 2
        cnt_r = jnp.minimum(grow + (w - back), rows) - jnp.maximum(grow - back, 0)
        cnt_c = jnp.minimum(gcol + (w - back), GRID_W) - jnp.maximum(gcol - back, 0)
        inv = 1.0 / (cnt_r * cnt_c).astype(F32)
        m = acc * jnp.concatenate([inv] * (POOL_GROUP_DIM // LANES), axis=1)
        _pool_project(m, cur_ref[0, :, cols].astype(F32), g, wp_ref, bp_ref, ps_ref, y_ref)


def _pool2d_call(z, w_pool, b_pool, pool_scale, rows):
    b, l, n = z.shape
    width = w_pool.shape[0] * POOL_GROUP_DIM
    p_blk = ACT_POOL
    assert rows % POOL_TILE_ROWS == 0 and POOL_TILE_ROWS % POOL_HALO_ROWS == 0
    assert max(POOL_WINDOWS) // 2 <= POOL_HALO_ROWS
    tile = POOL_TILE_ROWS * GRID_W
    halo = POOL_HALO_ROWS * GRID_W
    n_halo = l // halo
    per = tile // halo
    vec = pl.BlockSpec((1, width), lambda bi, i: (0, 0))
    return pl.pallas_call(
        functools.partial(_pool2d_kernel, rows=rows),
        out_shape=jax.ShapeDtypeStruct((b, l, width), BF16),
        grid=(b, l // tile),
        in_specs=[
            pl.BlockSpec((1, halo, width), lambda bi, i: (bi, jnp.maximum(i * per - 1, 0), p_blk)),
            pl.BlockSpec((1, tile, width), lambda bi, i: (bi, i, p_blk)),
            pl.BlockSpec((1, halo, width),
                         lambda bi, i: (bi, jnp.minimum((i + 1) * per, n_halo - 1), p_blk)),
            pl.BlockSpec(w_pool.shape, lambda bi, i: (0, 0, 0)),
            vec, vec,
        ],
        out_specs=pl.BlockSpec((1, tile, width), lambda bi, i: (bi, i, 0)),
        scratch_shapes=[pltpu.VMEM((halo + tile + halo, POOL_GROUP_DIM), F32)],
        compiler_params=_params("parallel", "parallel"),
        name="pool_mixer_2d",
    )(z, z, z, w_pool, b_pool, pool_scale)


def _pool1d_kernel(p_ref, wp_ref, bp_ref, ps_ref, y_ref):
    l = p_ref.shape[1]
    to = lax.broadcasted_iota(jnp.int32, (l, l), 0)
    dt = lax.broadcasted_iota(jnp.int32, (l, l), 1) - to
    pos = lax.broadcasted_iota(jnp.int32, (l, POOL_GROUP_DIM), 0)
    for g, w in enumerate(POOL_WINDOWS):
        cols = slice(g * POOL_GROUP_DIM, (g + 1) * POOL_GROUP_DIM)
        back = w // 2
        band = jnp.where((dt >= -back) & (dt < w - back), 1.0, 0.0).astype(BF16)
        v = p_ref[0, :, cols]
        total = jnp.dot(band, v, preferred_element_type=F32)
        cnt = jnp.minimum(pos + (w - back), l) - jnp.maximum(pos - back, 0)
        _pool_project(total / cnt.astype(F32), v.astype(F32), g, wp_ref, bp_ref, ps_ref, y_ref)


def _pool1d_call(z, w_pool, b_pool, pool_scale):
    b, l, n = z.shape
    width = w_pool.shape[0] * POOL_GROUP_DIM
    p_blk = ACT_POOL
    vec = pl.BlockSpec((1, width), lambda bi: (0, 0))
    return pl.pallas_call(
        _pool1d_kernel,
        out_shape=jax.ShapeDtypeStruct((b, l, width), BF16),
        grid=(b,),
        in_specs=[pl.BlockSpec((1, l, width), lambda bi: (bi, 0, p_blk)),
                  pl.BlockSpec(w_pool.shape, lambda bi: (0, 0, 0)), vec, vec],
        out_specs=pl.BlockSpec((1, l, width), lambda bi: (bi, 0, 0)),
        compiler_params=_params("parallel"),
        name="pool_mixer_1d",
    )(z, w_pool, b_pool, pool_scale)


def _outproj_kernel(of_ref, ob_ref, gate_ref, y_ref, x_ref, mod_ref, g_ref, hg_ref, w_ref, o_ref):
    width = of_ref.shape[2]
    o = of_ref[0].astype(F32) + ob_ref[0].astype(F32)
    heads = []
    for h in range(width // HEAD_DIM):
        heads.append(_rms(o[:, h * HEAD_DIM:(h + 1) * HEAD_DIM]) * hg_ref[...])
    og = (jnp.concatenate(heads, axis=-1) * gate_ref[0].astype(F32)).astype(BF16)
    mix = jnp.dot(jnp.concatenate([og, y_ref[0]], axis=-1), w_ref[...], preferred_element_type=F32)
    r = _rms(mix) * g_ref[pl.ds(3, 1), :]
    o_ref[0] = x_ref[0] + mod_ref[0, pl.ds(5, 1), :] * r


def _outproj_call(o_fw, o_bw, z, y, x, mod, gains, hgrn_gain, w, layer):
    b, l, d = x.shape
    width = o_fw.shape[2]
    g_blk = ACT_GATE
    tm = min(OUTPROJ_ROWS, l)
    tokw = lambda imap: pl.BlockSpec((1, tm, width), imap)
    row = lambda bi, i: (bi, i, 0)
    return pl.pallas_call(
        _outproj_kernel,
        out_shape=jax.ShapeDtypeStruct((b, l, d), F32),
        grid=(b, l // tm),
        in_specs=[
            tokw(row), tokw(row), tokw(lambda bi, i: (bi, i, g_blk)), tokw(row),
            pl.BlockSpec((1, tm, d), row),
            pl.BlockSpec((1, N_MOD, d), lambda bi, i: (bi, 0, 0)),
            pl.BlockSpec(gains.shape, lambda bi, i: (0, 0)),
            pl.BlockSpec((1, HEAD_DIM), lambda bi, i: (0, 0)),
            pl.BlockSpec((None,) + w.shape[1:], lambda bi, i: (layer, 0, 0)),
        ],
        out_specs=pl.BlockSpec((1, tm, d), row),
        compiler_params=_params("parallel", "parallel"),
        name="mixer_out_proj",
    )(o_fw, o_bw, z, y, x, mod, gains, hgrn_gain, w)


def kernel(x, c, ctx, c_ctx, w_ada, b_ada, norm_gain, ffn_in, ffn_out, w_in, hgrn_lb, hgrn_gain,
           w_pool, b_pool, pool_scale, w_out):
    bsz, seq, d = x.shape
    depth = w_ada.shape[0]
    rows = seq // GRID_W
    width = hgrn_lb.shape[-1]
    heads = width // HEAD_DIM

    lbp = jax.nn.softmax(hgrn_lb.astype(F32), axis=0)
    lower = jnp.cumsum(lbp, axis=0) - lbp[0]

    cvec = jnp.zeros((MOD_ROWS, d), F32).at[:bsz].set(c).at[bsz].set(c_ctx)
    mod_all = _mod_call(cvec, w_ada, b_ada)

    ffn_in_b, ffn_out_b = ffn_in.astype(BF16), ffn_out.astype(BF16)
    w_in_b, w_out_b, w_pool_b = w_in.astype(BF16), w_out.astype(BF16), w_pool.astype(BF16)

    s0 = jnp.zeros((bsz, heads, HEAD_DIM, HEAD_DIM), F32)
    ctx_len = ctx.shape[1]
    ctx = ctx.reshape(1, bsz * ctx_len, d)
    per_sample = lambda a: a.reshape(bsz, ctx_len, a.shape[-1])
    flat = lambda a: a.reshape(1, bsz * ctx_len, a.shape[-1])
    for l in range(depth):
        last = l == depth - 1
        gains = norm_gain[l]
        mod_x = mod_all[l, :bsz].reshape(bsz, N_MOD, d)
        mod_c = mod_all[l, bsz].reshape(1, N_MOD, d)
        hg = hgrn_gain[l].reshape(1, HEAD_DIM)
        bp = b_pool[l].reshape(1, -1)
        ps = pool_scale[l].reshape(1, -1)

        x = _ffn_call(x, mod_x, gains, ffn_in_b, ffn_out_b, l, 0)
        ctx = _ffn_call(ctx, mod_c, gains, ffn_in_b, ffn_out_b, l, 0)

        z_x, zg_x = _inproj_call(x, mod_x, gains, w_in_b, l, lower[l])
        z_c, zg_c = _inproj_call(ctx, mod_c, gains, w_in_b, l, lower[l])
        oc_fw, oc_bw, s_fw, s_bw = _hgrn_call(per_sample(z_c), per_sample(zg_c), s0, s0)
        ox_fw, ox_bw, _, _ = _hgrn_call(z_x, zg_x, s_fw, s_bw)
        y_x = _pool2d_call(z_x, w_pool_b[l], bp, ps, rows)
        x = _outproj_call(ox_fw, ox_bw, z_x, y_x, x, mod_x, gains, hg, w_out_b, l)
        if not last:
            y_c = _pool1d_call(per_sample(z_c), w_pool_b[l], bp, ps)
            ctx = _outproj_call(flat(oc_fw), flat(oc_bw), z_c, flat(y_c), ctx, mod_c, gains, hg, w_out_b, l)
            ctx = _ffn_call(ctx, mod_c, gains, ffn_in_b, ffn_out_b, l, 1)

        x = _ffn_call(x, mod_x, gains, ffn_in_b, ffn_out_b, l, 1)
    return x
```
